```python
import jax, jax.numpy as jnp
from jax import lax
import numpy as np

D_MODEL = 1024
BATCH = 8
SEQ = 4096
DEPTH = 4

N_MIXERS = 2
N_A = (DEPTH + 1) // 2
N_B = DEPTH // 2
WIDTH = 3 * D_MODEL // 2
LRU_HEADS = 12
LRU_BLOCK = WIDTH // LRU_HEADS
CONV_A = 4
CONV_B = 3
LRU_C = 8.0
NORM_EPS = 1e-6

kernel_name = "hybrid_rglru_shortconv_trunk"


def rmsnorm(x, g):
    xf = x.astype(jnp.float32)
    inv = lax.rsqrt(jnp.mean(xf * xf, axis=-1, keepdims=True) + NORM_EPS)
    return (xf * inv * g.astype(jnp.float32)).astype(x.dtype)


def causal_depthwise_conv(u, w, b=None):
    k_width = w.shape[0]
    s = u.shape[1]
    up = jnp.pad(u, ((0, 0), (k_width - 1, 0), (0, 0)))
    out = up[:, 0:s] * w[0]
    for k in range(1, k_width):
        out = out + up[:, k:k + s] * w[k]
    if b is not None:
        out = out + b
    return out


def rg_lru(u, gate_w, gate_b, lam):
    bsz, s, w = u.shape
    uh = u.reshape(bsz, s, LRU_HEADS, LRU_BLOCK)
    g = jnp.einsum('bshi,hio->bsho', uh, gate_w) + gate_b
    g = jax.nn.sigmoid(g.astype(jnp.float32))
    r = g[..., :LRU_BLOCK].reshape(bsz, s, w)
    i = g[..., LRU_BLOCK:].reshape(bsz, s, w)
    log_a = -LRU_C * r * jax.nn.softplus(-lam.astype(jnp.float32))
    a = jnp.exp(log_a)
    mult = jnp.sqrt(-jnp.expm1(2.0 * log_a))
    bterm = mult * (i * u.astype(jnp.float32))

    def combine(left, right):
        a_l, b_l = left
        a_r, b_r = right
        return a_l * a_r, a_r * b_l + b_r

    _, h = lax.associative_scan(combine, (a, bterm), axis=1)
    return h.astype(u.dtype)


def recurrent_mixer(h, w_in, conv_w, conv_b, gate_w, gate_b, lam, w_out):
    u = jnp.einsum('bsd,dc->bsc', h, w_in)
    gate, xb = u[..., :WIDTH], u[..., WIDTH:]
    xb = causal_depthwise_conv(xb, conv_w, conv_b)
    y = rg_lru(xb, gate_w, gate_b, lam)
    return jnp.einsum('bsc,cd->bsd', y * jax.nn.silu(gate), w_out)


def shortconv_mixer(h, w_in, conv_w, w_out):
    u = jnp.einsum('bsd,dc->bsc', h, w_in)
    bg = u[..., :WIDTH]
    cg = u[..., WIDTH:2 * WIDTH]
    xv = u[..., 2 * WIDTH:3 * WIDTH]
    gate = u[..., 3 * WIDTH:]
    y = bg * causal_depthwise_conv(cg * xv, conv_w)
    return jnp.einsum('bsc,cd->bsd', y * jax.nn.silu(gate), w_out)


def setup_inputs(seed: int = 0) -> dict:
    key = jax.random.key(seed)
    ks = jax.random.split(key, 16)
    f32 = jnp.float32
    x = jax.random.normal(ks[0], (BATCH, SEQ, D_MODEL), f32)
    norm_g = 1.0 + 0.02 * jax.random.normal(ks[1], (DEPTH, D_MODEL), f32)
    a_w_in = jax.random.normal(ks[2], (N_A, D_MODEL, 2 * WIDTH), f32) * D_MODEL ** -0.5
    a_conv_w = jax.random.normal(ks[3], (N_A, CONV_A, WIDTH), f32) * CONV_A ** -0.5
    a_conv_b = 0.02 * jax.random.normal(ks[4], (N_A, WIDTH), f32)
    a_gate_w = jax.random.normal(ks[5], (N_A, LRU_HEADS, LRU_BLOCK, 2 * LRU_BLOCK), f32) * LRU_BLOCK ** -0.5
    a_gate_b = 0.02 * jax.random.normal(ks[6], (N_A, LRU_HEADS, 2 * LRU_BLOCK), f32)
    a0 = jax.random.uniform(ks[7], (N_A, WIDTH), f32, minval=0.9, maxval=0.999)
    a_lambda = jnp.log(a0) - jnp.log1p(-a0)
    a_w_out = jax.random.normal(ks[8], (N_A, WIDTH, D_MODEL), f32) * WIDTH ** -0.5
    b_w_in = jax.random.normal(ks[9], (N_B, D_MODEL, 4 * WIDTH), f32) * D_MODEL ** -0.5
    b_conv_w = jax.random.normal(ks[10], (N_B, CONV_B, WIDTH), f32) * CONV_B ** -0.5
    b_w_out = jax.random.normal(ks[11], (N_B, WIDTH, D_MODEL), f32) * WIDTH ** -0.5
    final_g = 1.0 + 0.02 * jax.random.normal(ks[12], (D_MODEL,), f32)
    return {"x": x, "norm_g": norm_g, "a_w_in": a_w_in, "a_conv_w": a_conv_w,
            "a_conv_b": a_conv_b, "a_gate_w": a_gate_w, "a_gate_b": a_gate_b,
            "a_lambda": a_lambda, "a_w_out": a_w_out, "b_w_in": b_w_in,
            "b_conv_w": b_conv_w, "b_w_out": b_w_out, "final_g": final_g}


def reference(x, norm_g, a_w_in, a_conv_w, a_conv_b, a_gate_w, a_gate_b, a_lambda,
              a_w_out, b_w_in, b_conv_w, b_w_out, final_g):
    for i in range(DEPTH):
        h = rmsnorm(x, norm_g[i])
        j = i // N_MIXERS
        if i % N_MIXERS == 0:
            x = x + recurrent_mixer(h, a_w_in[j], a_conv_w[j], a_conv_b[j], a_gate_w[j],
                                    a_gate_b[j], a_lambda[j], a_w_out[j])
        else:
            x = x + shortconv_mixer(h, b_w_in[j], b_conv_w[j], b_w_out[j])
    return rmsnorm(x, final_g)
```

```python
import functools

import jax
import jax.numpy as jnp
from jax import lax
from jax.experimental import pallas as pl
from jax.experimental.pallas import tpu as pltpu

D_MODEL = 1024
BATCH = 8
DEPTH = 4
WIDTH = 3 * D_MODEL // 2
LRU_HEADS = 12
LRU_BLOCK = WIDTH // LRU_HEADS
CONV_A = 4
CONV_B = 3
LRU_C = 8.0
NORM_EPS = 1e-6

TILE_T = 64
TILE_M = TILE_T * BATCH
CHUNK = 256
HALO_A = (CONV_A - 1) * BATCH
HALO_B = (CONV_B - 1) * BATCH
VMEM_LIMIT_BYTES = 56 * 1024 * 1024


def _dot(a, b):
    return jnp.dot(a, b, preferred_element_type=jnp.float32)


def _sigmoid(v):
    return 0.5 * jnp.tanh(0.5 * v) + 0.5


def _rmsnorm(x, g):
    inv = lax.rsqrt(jnp.mean(x * x, axis=-1, keepdims=True) + NORM_EPS)
    return x * inv * g


def _layer_a_kernel(x_ref, g_ref, win_ref, cw_ref, cb_ref, gw_ref, gb_ref, lam_ref,
                    wout_ref, fg_ref, o_ref, xb_ref, a_ref, b_ref, sg_ref, hs_ref,
                    *, final):
    @pl.when(pl.program_id(0) == 0)
    def _():
        xb_ref[0:HALO_A, :] = jnp.zeros((HALO_A, WIDTH), jnp.float32)
        hs_ref[...] = jnp.zeros_like(hs_ref)

    x = x_ref[...]
    h = _rmsnorm(x, g_ref[...]).astype(jnp.bfloat16)

    nl = -lam_ref[...]
    softplus = jnp.maximum(nl, 0.0) + jnp.log1p(jnp.exp(-jnp.abs(nl)))
    neg_log_a_scale = LRU_C * softplus

    for c in range(WIDTH // CHUNK):
        lo = c * CHUNK
        gate = _dot(h, win_ref[:, lo:lo + CHUNK])
        xb_ref[HALO_A:, lo:lo + CHUNK] = _dot(h, win_ref[:, WIDTH + lo:WIDTH + lo + CHUNK])
        conv = xb_ref[0:TILE_M, lo:lo + CHUNK] * cw_ref[0:1, lo:lo + CHUNK]
        for k in range(1, CONV_A):
            conv = conv + xb_ref[k * BATCH:k * BATCH + TILE_M, lo:lo + CHUNK] * cw_ref[k:k + 1, lo:lo + CHUNK]
        conv = conv + cb_ref[:, lo:lo + CHUNK]
        xb_ref[0:HALO_A, lo:lo + CHUNK] = xb_ref[TILE_M:TILE_M + HALO_A, lo:lo + CHUNK]
        for hh in range(CHUNK // LRU_BLOCK):
            head = c * (CHUNK // LRU_BLOCK) + hh
            hl = lo + hh * LRU_BLOCK
            u = conv[:, hh * LRU_BLOCK:(hh + 1) * LRU_BLOCK]
            g2 = _dot(u.astype(jnp.bfloat16), gw_ref[head]) + gb_ref[head]
            r = _sigmoid(g2[:, :LRU_BLOCK])
            ig = _sigmoid(g2[:, LRU_BLOCK:])
            neg_log_a = r * neg_log_a_scale[:, hl:hl + LRU_BLOCK]
            a = jnp.exp(-neg_log_a)
            mult = jnp.sqrt(jnp.tanh(neg_log_a) * (1.0 + a * a))
            a_ref[:, hl:hl + LRU_BLOCK] = a
            b_ref[:, hl:hl + LRU_BLOCK] = mult * (ig * u)
        sg_ref[:, lo:lo + CHUNK] = gate * _sigmoid(gate)

    hs = hs_ref[...]
    for t in range(TILE_T):
        rows = slice(t * BATCH, (t + 1) * BATCH)
        hs = a_ref[rows, :] * hs + b_ref[rows, :]
        b_ref[rows, :] = hs
    hs_ref[...] = hs

    z = (b_ref[...] * sg_ref[...]).astype(jnp.bfloat16)
    out = x + _dot(z, wout_ref[...])
    if final:
        out = _rmsnorm(out, fg_ref[...])
    o_ref[...] = out


def _layer_b_kernel(x_ref, g_ref, win_ref, cw_ref, wout_ref, fg_ref, o_ref, p_ref, z_ref,
                    *, final):
    @pl.when(pl.program_id(0) == 0)
    def _():
        p_ref[0:HALO_B, :] = jnp.zeros((HALO_B, WIDTH), jnp.float32)

    x = x_ref[...]
    h = _rmsnorm(x, g_ref[...]).astype(jnp.bfloat16)

    for c in range(WIDTH // CHUNK):
        lo = c * CHUNK
        bg = _dot(h, win_ref[:, lo:lo + CHUNK])
        cg = _dot(h, win_ref[:, WIDTH + lo:WIDTH + lo + CHUNK])
        xv = _dot(h, win_ref[:, 2 * WIDTH + lo:2 * WIDTH + lo + CHUNK])
        gate = _dot(h, win_ref[:, 3 * WIDTH + lo:3 * WIDTH + lo + CHUNK])
        p_ref[HALO_B:, lo:lo + CHUNK] = cg * xv
        conv = p_ref[0:TILE_M, lo:lo + CHUNK] * cw_ref[0:1, lo:lo + CHUNK]
        for k in range(1, CONV_B):
            conv = conv + p_ref[k * BATCH:k * BATCH + TILE_M, lo:lo + CHUNK] * cw_ref[k:k + 1, lo:lo + CHUNK]
        p_ref[0:HALO_B, lo:lo + CHUNK] = p_ref[TILE_M:TILE_M + HALO_B, lo:lo + CHUNK]
        y = bg * conv
        z_ref[:, lo:lo + CHUNK] = (y * (gate * _sigmoid(gate))).astype(jnp.bfloat16)

    out = x + _dot(z_ref[...], wout_ref[...])
    if final:
        out = _rmsnorm(out, fg_ref[...])
    o_ref[...] = out


def _resident(shape):
    zeros = (0,) * len(shape)
    return pl.BlockSpec(shape, lambda i: zeros, pipeline_mode=pl.Buffered(1))


def _row_tile():
    return pl.BlockSpec((TILE_M, D_MODEL), lambda i: (i, 0))


def _compiler_params():
    return pltpu.CompilerParams(dimension_semantics=("arbitrary",),
                                vmem_limit_bytes=VMEM_LIMIT_BYTES)


def _layer_a(x, g, w_in, cw, cb, gw, gb, lam, w_out, fg, final):
    rows = x.shape[0]
    return pl.pallas_call(
        functools.partial(_layer_a_kernel, final=final),
        grid=(rows // TILE_M,),
        in_specs=[_row_tile(), _resident(g.shape), _resident(w_in.shape), _resident(cw.shape),
                  _resident(cb.shape), _resident(gw.shape), _resident(gb.shape),
                  _resident(lam.shape), _resident(w_out.shape), _resident(fg.shape)],
        out_specs=_row_tile(),
        out_shape=jax.ShapeDtypeStruct(x.shape, x.dtype),
        scratch_shapes=[pltpu.VMEM((HALO_A + TILE_M, WIDTH), jnp.float32),
                        pltpu.VMEM((TILE_M, WIDTH), jnp.float32),
                        pltpu.VMEM((TILE_M, WIDTH), jnp.float32),
                        pltpu.VMEM((TILE_M, WIDTH), jnp.float32),
                        pltpu.VMEM((BATCH, WIDTH), jnp.float32)],
        compiler_params=_compiler_params(),
        name="rglru_layer",
    )(x, g, w_in, cw, cb, gw, gb, lam, w_out, fg)


def _layer_b(x, g, w_in, cw, w_out, fg, final):
    rows = x.shape[0]
    return pl.pallas_call(
        functools.partial(_layer_b_kernel, final=final),
        grid=(rows // TILE_M,),
        in_specs=[_row_tile(), _resident(g.shape), _resident(w_in.shape), _resident(cw.shape),
                  _resident(w_out.shape), _resident(fg.shape)],
        out_specs=_row_tile(),
        out_shape=jax.ShapeDtypeStruct(x.shape, x.dtype),
        scratch_shapes=[pltpu.VMEM((HALO_B + TILE_M, WIDTH), jnp.float32),
                        pltpu.VMEM((TILE_M, WIDTH), jnp.bfloat16)],
        compiler_params=_compiler_params(),
        name="shortconv_layer",
    )(x, g, w_in, cw, w_out, fg)


def kernel(x, norm_g, a_w_in, a_conv_w, a_conv_b, a_gate_w, a_gate_b, a_lambda, a_w_out,
           b_w_in, b_conv_w, b_w_out, final_g):
    bsz, seq, d = x.shape
    assert (bsz, d) == (BATCH, D_MODEL) and seq % TILE_T == 0
    bf16 = jnp.bfloat16
    fg = final_g.reshape(1, d)
    xt = x.transpose(1, 0, 2).reshape(seq * bsz, d)
    for i in range(DEPTH):
        j = i // 2
        g = norm_g[i].reshape(1, d)
        final = i == DEPTH - 1
        if i % 2 == 0:
            xt = _layer_a(xt, g, a_w_in[j].astype(bf16), a_conv_w[j], a_conv_b[j].reshape(1, WIDTH),
                          a_gate_w[j].astype(bf16), a_gate_b[j].reshape(LRU_HEADS, 1, 2 * LRU_BLOCK),
                          a_lambda[j].reshape(1, WIDTH), a_w_out[j].astype(bf16), fg, final)
        else:
            xt = _layer_b(xt, g, b_w_in[j].astype(bf16), b_conv_w[j], b_w_out[j].astype(bf16), fg, final)
    return xt.reshape(seq, bsz, d).transpose(1, 0, 2)
```

```python
import functools

import jax
import jax.numpy as jnp
from jax import lax
from jax.experimental import pallas as pl
from jax.experimental.pallas import tpu as pltpu

D_MODEL = 1024
BATCH = 8
DEPTH = 4
WIDTH = 3 * D_MODEL // 2
LRU_HEADS = 12
LRU_BLOCK = WIDTH // LRU_HEADS
CONV_A = 4
CONV_B = 3
LRU_C = 8.0
NORM_EPS = 1e-6

TILE_T = 64
TILE_M = TILE_T * BATCH
CHUNK = 256
HALO_A = (CONV_A - 1) * BATCH
HALO_B = (CONV_B - 1) * BATCH
BLOCK_ROWS = 64
NORM_ROWS = 32
LOG2_E = 1.4426950408889634
LN_2 = 0.6931471805599453
VMEM_LIMIT_BYTES = 56 * 1024 * 1024


def _dot(a, b):
    return jnp.dot(a, b, preferred_element_type=jnp.float32)


def _sigmoid(v):
    return 0.5 * jnp.tanh(0.5 * v) + 0.5


def _rmsnorm(x, g):
    inv = lax.rsqrt(jnp.mean(x * x, axis=-1, keepdims=True) + NORM_EPS)
    return x * inv * g


def _rmsnorm_rows(x_ref, g_ref, dst_ref, dtype):
    for r0 in range(0, TILE_M, NORM_ROWS):
        rows = slice(r0, r0 + NORM_ROWS)
        dst_ref[rows, :] = _rmsnorm(x_ref[rows, :], g_ref[...]).astype(dtype)


def _layer_a_kernel(x_ref, g_ref, win_ref, cw_ref, cb_ref, gw_ref, gb_ref, lam_ref,
                    wout_ref, fg_ref, o_ref,
                    h_ref, gate_ref, xb_ref, uh_ref, a_ref, b_ref, z_ref, hs_ref, rowc_ref, gbh_ref,
                    *, final):
    @pl.when(pl.program_id(0) == 0)
    def _():
        xb_ref[0:HALO_A, :] = jnp.zeros((HALO_A, WIDTH), jnp.float32)
        hs_ref[...] = jnp.zeros_like(hs_ref)
        nl = -lam_ref[...]
        softplus = jnp.maximum(nl, 0.0) + jnp.log1p(jnp.exp(-jnp.abs(nl)))
        rowc_ref[0:1, :] = (-0.5 * LRU_C * LOG2_E) * softplus
        rowc_ref[1:1 + CONV_A, :] = 0.5 * cw_ref[...]
        rowc_ref[1 + CONV_A:2 + CONV_A, :] = 0.5 * cb_ref[...]
        gbh_ref[...] = 0.5 * gb_ref[...]

    _rmsnorm_rows(x_ref, g_ref, h_ref, jnp.bfloat16)

    def in_proj(c):
        lo = c * CHUNK
        gate_ref[:, lo:lo + CHUNK] = _dot(h_ref[...], win_ref[:, lo:lo + CHUNK])
        xb_ref[HALO_A:, lo:lo + CHUNK] = _dot(h_ref[...], win_ref[:, WIDTH + lo:WIDTH + lo + CHUNK])

    def conv(c):
        for hl in range(c * CHUNK, (c + 1) * CHUNK, LRU_BLOCK):
            cols = slice(hl, hl + LRU_BLOCK)
            for r0 in range(0, TILE_M, BLOCK_ROWS):
                win = xb_ref[r0:r0 + BLOCK_ROWS + HALO_A, cols]
                acc = win[0:BLOCK_ROWS] * rowc_ref[1:2, cols]
                for k in range(1, CONV_A):
                    acc = acc + win[k * BATCH:k * BATCH + BLOCK_ROWS] * rowc_ref[1 + k:2 + k, cols]
                uh_ref[r0:r0 + BLOCK_ROWS, cols] = acc + rowc_ref[1 + CONV_A:2 + CONV_A, cols]
            xb_ref[0:HALO_A, cols] = xb_ref[TILE_M:TILE_M + HALO_A, cols]

    def gate_proj(c):
        for hl in range(c * CHUNK, (c + 1) * CHUNK, LRU_BLOCK):
            cols = slice(hl, hl + LRU_BLOCK)
            g2 = _dot(uh_ref[:, cols].astype(jnp.bfloat16), gw_ref[hl // LRU_BLOCK])
            a_ref[:, cols] = g2[:, :LRU_BLOCK]
            b_ref[:, cols] = g2[:, LRU_BLOCK:]

    def gates(c):
        for hl in range(c * CHUNK, (c + 1) * CHUNK, LRU_BLOCK):
            cols = slice(hl, hl + LRU_BLOCK)
            gcols = slice(2 * hl, 2 * hl + LRU_BLOCK)
            icols = slice(2 * hl + LRU_BLOCK, 2 * hl + 2 * LRU_BLOCK)
            for r0 in range(0, TILE_M, BLOCK_ROWS):
                rows = slice(r0, r0 + BLOCK_ROWS)
                t_r = jnp.tanh(a_ref[rows, cols] + gbh_ref[:, gcols])
                t_i = jnp.tanh(b_ref[rows, cols] + gbh_ref[:, icols])
                k2 = rowc_ref[0:1, cols]
                log2_a = t_r * k2 + k2
                a = jnp.exp2(log2_a)
                mult = jnp.sqrt(jnp.tanh(log2_a * (-LN_2)) * (1.0 + a * a))
                uh = uh_ref[rows, cols]
                a_ref[rows, cols] = a
                b_ref[rows, cols] = mult * (t_i * uh + uh)
                gh = 0.5 * gate_ref[rows, cols]
                gate_ref[rows, cols] = gh * jnp.tanh(gh) + gh

    n_chunks = WIDTH // CHUNK
    in_proj(0)
    conv(0)
    gate_proj(0)
    for c in range(n_chunks):
        if c + 1 < n_chunks:
            in_proj(c + 1)
        gates(c)
        if c + 1 < n_chunks:
            conv(c + 1)
            gate_proj(c + 1)

    hs = hs_ref[...]
    for t in range(0, TILE_T, 2):
        r0 = t * BATCH
        h0 = a_ref[r0:r0 + BATCH, :] * hs + b_ref[r0:r0 + BATCH, :]
        hs = a_ref[r0 + BATCH:r0 + 2 * BATCH, :] * h0 + b_ref[r0 + BATCH:r0 + 2 * BATCH, :]
        y = jnp.concatenate([h0, hs], axis=0)
        z_ref[r0:r0 + 2 * BATCH, :] = (y * gate_ref[r0:r0 + 2 * BATCH, :]).astype(jnp.bfloat16)
    hs_ref[...] = hs

    o_ref[...] = x_ref[...] + _dot(z_ref[...], wout_ref[...])
    if final:
        _rmsnorm_rows(o_ref, fg_ref, o_ref, jnp.float32)


def _layer_b_kernel(x_ref, g_ref, win_ref, cw_ref, wout_ref, fg_ref, o_ref, p_ref, z_ref,
                    *, final):
    @pl.when(pl.program_id(0) == 0)
    def _():
        p_ref[0:HALO_B, :] = jnp.zeros((HALO_B, WIDTH), jnp.float32)

    x = x_ref[...]
    h = _rmsnorm(x, g_ref[...]).astype(jnp.bfloat16)

    for c in range(WIDTH // CHUNK):
        lo = c * CHUNK
        bg = _dot(h, win_ref[:, lo:lo + CHUNK])
        cg = _dot(h, win_ref[:, WIDTH + lo:WIDTH + lo + CHUNK])
        xv = _dot(h, win_ref[:, 2 * WIDTH + lo:2 * WIDTH + lo + CHUNK])
        gate = _dot(h, win_ref[:, 3 * WIDTH + lo:3 * WIDTH + lo + CHUNK])
        p_ref[HALO_B:, lo:lo + CHUNK] = cg * xv
        conv = p_ref[0:TILE_M, lo:lo + CHUNK] * cw_ref[0:1, lo:lo + CHUNK]
        for k in range(1, CONV_B):
            conv = conv + p_ref[k * BATCH:k * BATCH + TILE_M, lo:lo + CHUNK] * cw_ref[k:k + 1, lo:lo + CHUNK]
        p_ref[0:HALO_B, lo:lo + CHUNK] = p_ref[TILE_M:TILE_M + HALO_B, lo:lo + CHUNK]
        y = bg * conv
        z_ref[:, lo:lo + CHUNK] = (y * (gate * _sigmoid(gate))).astype(jnp.bfloat16)

    out = x + _dot(z_ref[...], wout_ref[...])
    if final:
        out = _rmsnorm(out, fg_ref[...])
    o_ref[...] = out


def _resident(shape):
    zeros = (0,) * len(shape)
    return pl.BlockSpec(shape, lambda i: zeros, pipeline_mode=pl.Buffered(1))


def _row_tile():
    return pl.BlockSpec((TILE_M, D_MODEL), lambda i: (i, 0))


def _compiler_params():
    return pltpu.CompilerParams(dimension_semantics=("arbitrary",),
                                vmem_limit_bytes=VMEM_LIMIT_BYTES)


def _layer_a(x, g, w_in, cw, cb, gw, gb, lam, w_out, fg, final):
    rows = x.shape[0]
    return pl.pallas_call(
        functools.partial(_layer_a_kernel, final=final),
        grid=(rows // TILE_M,),
        in_specs=[_row_tile(), _resident(g.shape), _resident(w_in.shape), _resident(cw.shape),
                  _resident(cb.shape), _resident(gw.shape), _resident(gb.shape),
                  _resident(lam.shape), _resident(w_out.shape), _resident(fg.shape)],
        out_specs=_row_tile(),
        out_shape=jax.ShapeDtypeStruct(x.shape, x.dtype),
        scratch_shapes=[pltpu.VMEM((TILE_M, D_MODEL), jnp.bfloat16),
                        pltpu.VMEM((TILE_M, WIDTH), jnp.float32),
                        pltpu.VMEM((HALO_A + TILE_M, WIDTH), jnp.float32),
                        pltpu.VMEM((TILE_M, WIDTH), jnp.float32),
                        pltpu.VMEM((TILE_M, WIDTH), jnp.float32),
                        pltpu.VMEM((TILE_M, WIDTH), jnp.float32),
                        pltpu.VMEM((TILE_M, WIDTH), jnp.bfloat16),
                        pltpu.VMEM((BATCH, WIDTH), jnp.float32),
                        pltpu.VMEM((BATCH, WIDTH), jnp.float32),
                        pltpu.VMEM((1, 2 * WIDTH), jnp.float32)],
        compiler_params=_compiler_params(),
        name="rglru_layer",
    )(x, g, w_in, cw, cb, gw, gb, lam, w_out, fg)


def _layer_b(x, g, w_in, cw, w_out, fg, final):
    rows = x.shape[0]
    return pl.pallas_call(
        functools.partial(_layer_b_kernel, final=final),
        grid=(rows // TILE_M,),
        in_specs=[_row_tile(), _resident(g.shape), _resident(w_in.shape), _resident(cw.shape),
                  _resident(w_out.shape), _resident(fg.shape)],
        out_specs=_row_tile(),
        out_shape=jax.ShapeDtypeStruct(x.shape, x.dtype),
        scratch_shapes=[pltpu.VMEM((HALO_B + TILE_M, WIDTH), jnp.float32),
                        pltpu.VMEM((TILE_M, WIDTH), jnp.bfloat16)],
        compiler_params=_compiler_params(),
        name="shortconv_layer",
    )(x, g, w_in, cw, w_out, fg)


def kernel(x, norm_g, a_w_in, a_conv_w, a_conv_b, a_gate_w, a_gate_b, a_lambda, a_w_out,
           b_w_in, b_conv_w, b_w_out, final_g):
    bsz, seq, d = x.shape
    assert (bsz, d) == (BATCH, D_MODEL) and seq % TILE_T == 0
    bf16 = jnp.bfloat16
    fg = final_g.reshape(1, d)
    xt = x.transpose(1, 0, 2).reshape(seq * bsz, d)
    for i in range(DEPTH):
        j = i // 2
        g = norm_g[i].reshape(1, d)
        final = i == DEPTH - 1
        if i % 2 == 0:
            xt = _layer_a(xt, g, a_w_in[j].astype(bf16), a_conv_w[j], a_conv_b[j].reshape(1, WIDTH),
                          a_gate_w[j].astype(bf16), a_gate_b[j].reshape(1, 2 * WIDTH),
                          a_lambda[j].reshape(1, WIDTH), a_w_out[j].astype(bf16), fg, final)
        else:
            xt = _layer_b(xt, g, b_w_in[j].astype(bf16), b_conv_w[j], b_w_out[j].astype(bf16), fg, final)
    return xt.reshape(seq, bsz, d).transpose(1, 0, 2)
```

```python
import functools

import jax
import jax.numpy as jnp
from jax import lax
from jax.experimental import pallas as pl
from jax.experimental.pallas import tpu as pltpu

D_MODEL = 1024
BATCH = 8
DEPTH = 4
WIDTH = 3 * D_MODEL // 2
LRU_HEADS = 12
LRU_BLOCK = WIDTH // LRU_HEADS
CONV_A = 4
CONV_B = 3
LRU_C = 8.0
NORM_EPS = 1e-6

TILE_T = 64
TILE_M = TILE_T * BATCH
CHUNK = 256
HALO_A = (CONV_A - 1) * BATCH
HALO_B = (CONV_B - 1) * BATCH
BLOCK_ROWS = 64
NORM_ROWS = 32
LOG2_E = 1.4426950408889634
LN_2 = 0.6931471805599453
VMEM_LIMIT_BYTES = 56 * 1024 * 1024


def _dot(a, b):
    return jnp.dot(a, b, preferred_element_type=jnp.float32)


def _sigmoid(v):
    return 0.5 * jnp.tanh(0.5 * v) + 0.5


def _rmsnorm(x, g):
    inv = lax.rsqrt(jnp.mean(x * x, axis=-1, keepdims=True) + NORM_EPS)
    return x * inv * g


def _rmsnorm_rows(x_ref, g_ref, dst_ref, dtype):
    for r0 in range(0, TILE_M, NORM_ROWS):
        rows = slice(r0, r0 + NORM_ROWS)
        dst_ref[rows, :] = _rmsnorm(x_ref[rows, :], g_ref[...]).astype(dtype)


def _layer_a_kernel(x_ref, g_ref, win_ref, cw_ref, cb_ref, gw_ref, gb_ref, lam_ref,
                    wout_ref, fg_ref, o_ref,
                    h_ref, gate_ref, xb_ref, uh_ref, a_ref, b_ref, hs_ref, rowc_ref, gbh_ref,
                    *, final):
    @pl.when(pl.program_id(0) == 0)
    def _():
        xb_ref[0:HALO_A, :] = jnp.zeros((HALO_A, WIDTH), jnp.float32)
        hs_ref[...] = jnp.zeros_like(hs_ref)
        nl = -lam_ref[...]
        softplus = jnp.maximum(nl, 0.0) + jnp.log1p(jnp.exp(-jnp.abs(nl)))
        rowc_ref[0:1, :] = (-0.5 * LRU_C * LOG2_E) * softplus
        rowc_ref[1:1 + CONV_A, :] = 0.5 * cw_ref[...]
        rowc_ref[1 + CONV_A:2 + CONV_A, :] = 0.5 * cb_ref[...]
        gbh_ref[...] = 0.5 * gb_ref[...]

    _rmsnorm_rows(x_ref, g_ref, h_ref, jnp.float32)

    def in_proj(c):
        lo = c * CHUNK
        gate_ref[:, lo:lo + CHUNK] = _dot(h_ref[...], win_ref[:, lo:lo + CHUNK])
        xb_ref[HALO_A:, lo:lo + CHUNK] = _dot(h_ref[...], win_ref[:, WIDTH + lo:WIDTH + lo + CHUNK])

    def conv(c):
        for hl in range(c * CHUNK, (c + 1) * CHUNK, LRU_BLOCK):
            cols = slice(hl, hl + LRU_BLOCK)
            for r0 in range(0, TILE_M, BLOCK_ROWS):
                win = xb_ref[r0:r0 + BLOCK_ROWS + HALO_A, cols]
                acc = win[0:BLOCK_ROWS] * rowc_ref[1:2, cols]
                for k in range(1, CONV_A):
                    acc = acc + win[k * BATCH:k * BATCH + BLOCK_ROWS] * rowc_ref[1 + k:2 + k, cols]
                uh_ref[r0:r0 + BLOCK_ROWS, cols] = acc + rowc_ref[1 + CONV_A:2 + CONV_A, cols]
            xb_ref[0:HALO_A, cols] = xb_ref[TILE_M:TILE_M + HALO_A, cols]

    def gate_proj(c):
        for hl in range(c * CHUNK, (c + 1) * CHUNK, LRU_BLOCK):
            cols = slice(hl, hl + LRU_BLOCK)
            g2 = _dot(uh_ref[:, cols], gw_ref[hl // LRU_BLOCK])
            a_ref[:, cols] = g2[:, :LRU_BLOCK]
            b_ref[:, cols] = g2[:, LRU_BLOCK:]

    def gates(c):
        for hl in range(c * CHUNK, (c + 1) * CHUNK, LRU_BLOCK):
            cols = slice(hl, hl + LRU_BLOCK)
            gcols = slice(2 * hl, 2 * hl + LRU_BLOCK)
            icols = slice(2 * hl + LRU_BLOCK, 2 * hl + 2 * LRU_BLOCK)
            for r0 in range(0, TILE_M, BLOCK_ROWS):
                rows = slice(r0, r0 + BLOCK_ROWS)
                t_r = jnp.tanh(a_ref[rows, cols] + gbh_ref[:, gcols])
                t_i = jnp.tanh(b_ref[rows, cols] + gbh_ref[:, icols])
                k2 = rowc_ref[0:1, cols]
                log2_a = t_r * k2 + k2
                a = jnp.exp2(log2_a)
                mult = jnp.sqrt(jnp.tanh(log2_a * (-LN_2)) * (1.0 + a * a))
                uh = uh_ref[rows, cols]
                a_ref[rows, cols] = a
                b_ref[rows, cols] = mult * (t_i * uh + uh)
                gh = 0.5 * gate_ref[rows, cols]
                gate_ref[rows, cols] = gh * jnp.tanh(gh) + gh

    n_chunks = WIDTH // CHUNK
    in_proj(0)
    conv(0)
    gate_proj(0)
    for c in range(n_chunks):
        if c + 1 < n_chunks:
            in_proj(c + 1)
        gates(c)
        if c + 1 < n_chunks:
            conv(c + 1)
            gate_proj(c + 1)

    hs = hs_ref[...]
    for t in range(TILE_T):
        rows = slice(t * BATCH, (t + 1) * BATCH)
        hs = a_ref[rows, :] * hs + b_ref[rows, :]
        b_ref[rows, :] = hs * gate_ref[rows, :]
    hs_ref[...] = hs

    o_ref[...] = x_ref[...] + _dot(b_ref[...], wout_ref[...])
    if final:
        _rmsnorm_rows(o_ref, fg_ref, o_ref, jnp.float32)


def _layer_b_kernel(x_ref, g_ref, win_ref, cw_ref, wout_ref, fg_ref, o_ref, p_ref, z_ref,
                    *, final):
    @pl.when(pl.program_id(0) == 0)
    def _():
        p_ref[0:HALO_B, :] = jnp.zeros((HALO_B, WIDTH), jnp.float32)

    x = x_ref[...]
    h = _rmsnorm(x, g_ref[...])

    for c in range(WIDTH // CHUNK):
        lo = c * CHUNK
        bg = _dot(h, win_ref[:, lo:lo + CHUNK])
        cg = _dot(h, win_ref[:, WIDTH + lo:WIDTH + lo + CHUNK])
        xv = _dot(h, win_ref[:, 2 * WIDTH + lo:2 * WIDTH + lo + CHUNK])
        gate = _dot(h, win_ref[:, 3 * WIDTH + lo:3 * WIDTH + lo + CHUNK])
        p_ref[HALO_B:, lo:lo + CHUNK] = cg * xv
        conv = p_ref[0:TILE_M, lo:lo + CHUNK] * cw_ref[0:1, lo:lo + CHUNK]
        for k in range(1, CONV_B):
            conv = conv + p_ref[k * BATCH:k * BATCH + TILE_M, lo:lo + CHUNK] * cw_ref[k:k + 1, lo:lo + CHUNK]
        p_ref[0:HALO_B, lo:lo + CHUNK] = p_ref[TILE_M:TILE_M + HALO_B, lo:lo + CHUNK]
        y = bg * conv
        z_ref[:, lo:lo + CHUNK] = y * (gate * _sigmoid(gate))

    out = x + _dot(z_ref[...], wout_ref[...])
    if final:
        out = _rmsnorm(out, fg_ref[...])
    o_ref[...] = out


def _resident(shape):
    zeros = (0,) * len(shape)
    return pl.BlockSpec(shape, lambda i: zeros, pipeline_mode=pl.Buffered(1))


def _row_tile():
    return pl.BlockSpec((TILE_M, D_MODEL), lambda i: (i, 0))


def _compiler_params():
    return pltpu.CompilerParams(dimension_semantics=("arbitrary",),
                                vmem_limit_bytes=VMEM_LIMIT_BYTES)


def _layer_a(x, g, w_in, cw, cb, gw, gb, lam, w_out, fg, final):
    rows = x.shape[0]
    return pl.pallas_call(
        functools.partial(_layer_a_kernel, final=final),
        grid=(rows // TILE_M,),
        in_specs=[_row_tile(), _resident(g.shape), _resident(w_in.shape), _resident(cw.shape),
                  _resident(cb.shape), _resident(gw.shape), _resident(gb.shape),
                  _resident(lam.shape), _resident(w_out.shape), _resident(fg.shape)],
        out_specs=_row_tile(),
        out_shape=jax.ShapeDtypeStruct(x.shape, x.dtype),
        scratch_shapes=[pltpu.VMEM((TILE_M, D_MODEL), jnp.float32),
                        pltpu.VMEM((TILE_M, WIDTH), jnp.float32),
                        pltpu.VMEM((HALO_A + TILE_M, WIDTH), jnp.float32),
                        pltpu.VMEM((TILE_M, WIDTH), jnp.float32),
                        pltpu.VMEM((TILE_M, WIDTH), jnp.float32),
                        pltpu.VMEM((TILE_M, WIDTH), jnp.float32),
                        pltpu.VMEM((BATCH, WIDTH), jnp.float32),
                        pltpu.VMEM((BATCH, WIDTH), jnp.float32),
                        pltpu.VMEM((1, 2 * WIDTH), jnp.float32)],
        compiler_params=_compiler_params(),
        name="rglru_layer",
    )(x, g, w_in, cw, cb, gw, gb, lam, w_out, fg)


def _layer_b(x, g, w_in, cw, w_out, fg, final):
    rows = x.shape[0]
    return pl.pallas_call(
        functools.partial(_layer_b_kernel, final=final),
        grid=(rows // TILE_M,),
        in_specs=[_row_tile(), _resident(g.shape), _resident(w_in.shape), _resident(cw.shape),
                  _resident(w_out.shape), _resident(fg.shape)],
        out_specs=_row_tile(),
        out_shape=jax.ShapeDtypeStruct(x.shape, x.dtype),
        scratch_shapes=[pltpu.VMEM((HALO_B + TILE_M, WIDTH), jnp.float32),
                        pltpu.VMEM((TILE_M, WIDTH), jnp.float32)],
        compiler_params=_compiler_params(),
        name="shortconv_layer",
    )(x, g, w_in, cw, w_out, fg)


def kernel(x, norm_g, a_w_in, a_conv_w, a_conv_b, a_gate_w, a_gate_b, a_lambda, a_w_out,
           b_w_in, b_conv_w, b_w_out, final_g):
    bsz, seq, d = x.shape
    assert (bsz, d) == (BATCH, D_MODEL) and seq % TILE_T == 0
    fg = final_g.reshape(1, d)
    xt = x.transpose(1, 0, 2).reshape(seq * bsz, d)
    for i in range(DEPTH):
        j = i // 2
        g = norm_g[i].reshape(1, d)
        final = i == DEPTH - 1
        if i % 2 == 0:
            xt = _layer_a(xt, g, a_w_in[j], a_conv_w[j], a_conv_b[j].reshape(1, WIDTH),
                          a_gate_w[j], a_gate_b[j].reshape(1, 2 * WIDTH),
                          a_lambda[j].reshape(1, WIDTH), a_w_out[j], fg, final)
        else:
            xt = _layer_b(xt, g, b_w_in[j], b_conv_w[j], b_w_out[j], fg, final)
    return xt.reshape(seq, bsz, d).transpose(1, 0, 2)
```

```python
import functools

import jax
import jax.numpy as jnp
from jax import lax
from jax.experimental import pallas as pl
from jax.experimental.pallas import tpu as pltpu

D_MODEL = 1024
BATCH = 8
DEPTH = 4
WIDTH = 3 * D_MODEL // 2
LRU_HEADS = 12
LRU_BLOCK = WIDTH // LRU_HEADS
CONV_A = 4
CONV_B = 3
LRU_C = 8.0
NORM_EPS = 1e-6

TILE_T = 64
TILE_M = TILE_T * BATCH
CHUNK = 256
HALO_A = (CONV_A - 1) * BATCH
HALO_B = (CONV_B - 1) * BATCH
BLOCK_ROWS = 64
NORM_T = 4
NORM_ROWS = NORM_T * BATCH
LOG2_E = 1.4426950408889634
LN_2 = 0.6931471805599453
VMEM_LIMIT_BYTES = 56 * 1024 * 1024


def _dot(a, b):
    return jnp.dot(a, b, preferred_element_type=jnp.float32)


def _rmsnorm(x, g):
    inv = lax.rsqrt(jnp.mean(x * x, axis=-1, keepdims=True) + NORM_EPS)
    return x * inv * g


def _silu_from_half(gh):
    return gh * jnp.tanh(gh) + gh


def _rmsnorm_rows(x_ref, g_ref, dst_ref):
    for r0 in range(0, TILE_M, NORM_ROWS):
        rows = slice(r0, r0 + NORM_ROWS)
        dst_ref[rows, :] = _rmsnorm(x_ref[rows, :], g_ref[...])


def _to_time_major(x_ref, dst_ref):
    for t0 in range(0, TILE_T, BATCH):
        for lo in range(0, D_MODEL, LRU_BLOCK):
            v = x_ref[:, t0:t0 + BATCH, lo:lo + LRU_BLOCK]
            dst_ref[t0 * BATCH:(t0 + BATCH) * BATCH, lo:lo + LRU_BLOCK] = (
                jnp.swapaxes(v, 0, 1).reshape(BATCH * BATCH, LRU_BLOCK))


def _store_batch_major(rows, o_ref, t0, lo):
    v = rows.reshape(BATCH, BATCH, rows.shape[-1])
    o_ref[:, t0:t0 + BATCH, lo:lo + rows.shape[-1]] = jnp.swapaxes(v, 0, 1)


def _layer_a_kernel(x_ref, g_ref, win_ref, cw_ref, cb_ref, gw_ref, gb_ref, lam_ref, wout_ref, o_ref,
                    h_ref, gate_ref, xb_ref, uh_ref, a_ref, b_ref, hs_ref, rowc_ref, gbh_ref, *xtm,
                    first):
    @pl.when(pl.program_id(0) == 0)
    def _():
        xb_ref[0:HALO_A, :] = jnp.zeros((HALO_A, WIDTH), jnp.float32)
        hs_ref[...] = jnp.zeros_like(hs_ref)
        nl = -lam_ref[...]
        softplus = jnp.maximum(nl, 0.0) + jnp.log1p(jnp.exp(-jnp.abs(nl)))
        rowc_ref[0:1, :] = (-0.5 * LRU_C * LOG2_E) * softplus
        rowc_ref[1:1 + CONV_A, :] = 0.5 * cw_ref[...]
        rowc_ref[1 + CONV_A:2 + CONV_A, :] = 0.5 * cb_ref[...]
        gbh_ref[...] = 0.5 * gb_ref[...]

    if first:
        (x_rows_ref,) = xtm
        _to_time_major(x_ref, x_rows_ref)
    else:
        x_rows_ref = x_ref
    _rmsnorm_rows(x_rows_ref, g_ref, h_ref)

    def in_proj(c):
        lo = c * CHUNK
        gate_ref[:, lo:lo + CHUNK] = _dot(h_ref[...], win_ref[:, lo:lo + CHUNK])
        xb_ref[HALO_A:, lo:lo + CHUNK] = _dot(h_ref[...], win_ref[:, WIDTH + lo:WIDTH + lo + CHUNK])

    def conv(c):
        for hl in range(c * CHUNK, (c + 1) * CHUNK, LRU_BLOCK):
            cols = slice(hl, hl + LRU_BLOCK)
            for r0 in range(0, TILE_M, BLOCK_ROWS):
                win = xb_ref[r0:r0 + BLOCK_ROWS + HALO_A, cols]
                acc = win[0:BLOCK_ROWS] * rowc_ref[1:2, cols]
                for k in range(1, CONV_A):
                    acc = acc + win[k * BATCH:k * BATCH + BLOCK_ROWS] * rowc_ref[1 + k:2 + k, cols]
                uh_ref[r0:r0 + BLOCK_ROWS, cols] = acc + rowc_ref[1 + CONV_A:2 + CONV_A, cols]
            xb_ref[0:HALO_A, cols] = xb_ref[TILE_M:TILE_M + HALO_A, cols]

    def gate_proj(c):
        for hl in range(c * CHUNK, (c + 1) * CHUNK, LRU_BLOCK):
            cols = slice(hl, hl + LRU_BLOCK)
            g2 = _dot(uh_ref[:, cols], gw_ref[hl // LRU_BLOCK])
            a_ref[:, cols] = g2[:, :LRU_BLOCK]
            b_ref[:, cols] = g2[:, LRU_BLOCK:]

    def gates(c):
        for hl in range(c * CHUNK, (c + 1) * CHUNK, LRU_BLOCK):
            cols = slice(hl, hl + LRU_BLOCK)
            gcols = slice(2 * hl, 2 * hl + LRU_BLOCK)
            icols = slice(2 * hl + LRU_BLOCK, 2 * hl + 2 * LRU_BLOCK)
            for r0 in range(0, TILE_M, BLOCK_ROWS):
                rows = slice(r0, r0 + BLOCK_ROWS)
                t_r = jnp.tanh(a_ref[rows, cols] + gbh_ref[:, gcols])
                t_i = jnp.tanh(b_ref[rows, cols] + gbh_ref[:, icols])
                k2 = rowc_ref[0:1, cols]
                log2_a = t_r * k2 + k2
                a = jnp.exp2(log2_a)
                mult = jnp.sqrt(jnp.tanh(log2_a * (-LN_2)) * (1.0 + a * a))
                uh = uh_ref[rows, cols]
                a_ref[rows, cols] = a
                b_ref[rows, cols] = mult * (t_i * uh + uh)
                gate_ref[rows, cols] = _silu_from_half(0.5 * gate_ref[rows, cols])

    n_chunks = WIDTH // CHUNK
    in_proj(0)
    conv(0)
    gate_proj(0)
    for c in range(n_chunks):
        if c + 1 < n_chunks:
            in_proj(c + 1)
        gates(c)
        if c + 1 < n_chunks:
            conv(c + 1)
            gate_proj(c + 1)

    hs = hs_ref[...]
    for t in range(TILE_T):
        rows = slice(t * BATCH, (t + 1) * BATCH)
        hs = a_ref[rows, :] * hs + b_ref[rows, :]
        b_ref[rows, :] = hs * gate_ref[rows, :]
    hs_ref[...] = hs

    o_ref[...] = x_rows_ref[...] + _dot(b_ref[...], wout_ref[...])


def _layer_b_kernel(x_ref, g_ref, win_ref, cw_ref, wout_ref, fg_ref, o_ref, p_ref, z_ref, *rows_scratch,
                    final):
    @pl.when(pl.program_id(0) == 0)
    def _():
        p_ref[0:HALO_B, :] = jnp.zeros((HALO_B, WIDTH), jnp.float32)

    x = x_ref[...]
    h = _rmsnorm(x, g_ref[...])

    for c in range(WIDTH // CHUNK):
        lo = c * CHUNK
        bg = _dot(h, win_ref[:, lo:lo + CHUNK])
        cg = _dot(h, win_ref[:, WIDTH + lo:WIDTH + lo + CHUNK])
        xv = _dot(h, win_ref[:, 2 * WIDTH + lo:2 * WIDTH + lo + CHUNK])
        gate = _dot(h, win_ref[:, 3 * WIDTH + lo:3 * WIDTH + lo + CHUNK])
        p_ref[HALO_B:, lo:lo + CHUNK] = cg * xv
        conv = p_ref[0:TILE_M, lo:lo + CHUNK] * cw_ref[0:1, lo:lo + CHUNK]
        for k in range(1, CONV_B):
            conv = conv + p_ref[k * BATCH:k * BATCH + TILE_M, lo:lo + CHUNK] * cw_ref[k:k + 1, lo:lo + CHUNK]
        p_ref[0:HALO_B, lo:lo + CHUNK] = p_ref[TILE_M:TILE_M + HALO_B, lo:lo + CHUNK]
        z_ref[:, lo:lo + CHUNK] = (bg * conv) * _silu_from_half(0.5 * gate)

    out = x + _dot(z_ref[...], wout_ref[...])
    if not final:
        o_ref[...] = out
    else:
        (rows_ref,) = rows_scratch
        rows_ref[...] = out
        for t0 in range(0, TILE_T, BATCH):
            rows = slice(t0 * BATCH, (t0 + BATCH) * BATCH)
            v = rows_ref[rows, :]
            inv = lax.rsqrt(jnp.mean(v * v, axis=-1, keepdims=True) + NORM_EPS)
            for lo in range(0, D_MODEL, LRU_BLOCK):
                y = rows_ref[rows, lo:lo + LRU_BLOCK] * inv * fg_ref[:, lo:lo + LRU_BLOCK]
                _store_batch_major(y, o_ref, t0, lo)


def _resident(shape, *index):
    block = (None,) * len(index) + tuple(shape[len(index):])
    full_index = tuple(index) + (0,) * (len(shape) - len(index))
    return pl.BlockSpec(block, lambda i: full_index, pipeline_mode=pl.Buffered(1))


def _time_major_tile():
    return pl.BlockSpec((TILE_M, D_MODEL), lambda i: (i, 0))


def _batch_major_tile():
    return pl.BlockSpec((BATCH, TILE_T, D_MODEL), lambda i: (0, i, 0))


def _compiler_params():
    return pltpu.CompilerParams(dimension_semantics=("arbitrary",),
                                vmem_limit_bytes=VMEM_LIMIT_BYTES)


def _wide_scratch():
    return pltpu.VMEM((TILE_M, WIDTH), jnp.float32)


def _layer_a(x, j, layer, norm_g, w_in, cw, cb, gw, gb, lam, w_out, first):
    seq = x.shape[1] if first else x.shape[0] // BATCH
    scratch = [pltpu.VMEM((TILE_M, D_MODEL), jnp.float32),
               _wide_scratch(),
               pltpu.VMEM((HALO_A + TILE_M, WIDTH), jnp.float32),
               _wide_scratch(),
               _wide_scratch(),
               _wide_scratch(),
               pltpu.VMEM((BATCH, WIDTH), jnp.float32),
               pltpu.VMEM((BATCH, WIDTH), jnp.float32),
               pltpu.VMEM((1, 2 * WIDTH), jnp.float32)]
    if first:
        scratch.append(pltpu.VMEM((TILE_M, D_MODEL), jnp.float32))
    return pl.pallas_call(
        functools.partial(_layer_a_kernel, first=first),
        grid=(seq // TILE_T,),
        in_specs=[_batch_major_tile() if first else _time_major_tile(),
                  _resident(norm_g.shape, layer), _resident(w_in.shape, j), _resident(cw.shape, j),
                  _resident(cb.shape, j), _resident(gw.shape, j), _resident(gb.shape, j),
                  _resident(lam.shape, j), _resident(w_out.shape, j)],
        out_specs=_time_major_tile(),
        out_shape=jax.ShapeDtypeStruct((seq * BATCH, D_MODEL), x.dtype),
        scratch_shapes=scratch,
        compiler_params=_compiler_params(),
        name="rglru_layer",
    )(x, norm_g, w_in, cw, cb, gw, gb, lam, w_out)


def _layer_b(x, j, layer, norm_g, w_in, cw, w_out, fg, final):
    seq = x.shape[0] // BATCH
    out_shape = (BATCH, seq, D_MODEL) if final else x.shape
    scratch = [pltpu.VMEM((HALO_B + TILE_M, WIDTH), jnp.float32),
               _wide_scratch()]
    if final:
        scratch.append(pltpu.VMEM((TILE_M, D_MODEL), jnp.float32))
    return pl.pallas_call(
        functools.partial(_layer_b_kernel, final=final),
        grid=(seq // TILE_T,),
        in_specs=[_time_major_tile(), _resident(norm_g.shape, layer), _resident(w_in.shape, j),
                  _resident(cw.shape, j), _resident(w_out.shape, j), _resident(fg.shape)],
        out_specs=_batch_major_tile() if final else _time_major_tile(),
        out_shape=jax.ShapeDtypeStruct(out_shape, x.dtype),
        scratch_shapes=scratch,
        compiler_params=_compiler_params(),
        name="shortconv_layer",
    )(x, norm_g, w_in, cw, w_out, fg)


def kernel(x, norm_g, a_w_in, a_conv_w, a_conv_b, a_gate_w, a_gate_b, a_lambda, a_w_out,
           b_w_in, b_conv_w, b_w_out, final_g):
    bsz, seq, d = x.shape
    assert (bsz, d) == (BATCH, D_MODEL) and seq % TILE_T == 0 and DEPTH % 2 == 0
    n_a = a_w_in.shape[0]
    norm_g = norm_g.reshape(DEPTH, 1, d)
    a_conv_b = a_conv_b.reshape(n_a, 1, WIDTH)
    a_gate_b = a_gate_b.reshape(n_a, 1, 2 * WIDTH)
    a_lambda = a_lambda.reshape(n_a, 1, WIDTH)
    fg = final_g.reshape(1, d)
    for i in range(DEPTH):
        j = i // 2
        if i % 2 == 0:
            x = _layer_a(x, j, i, norm_g, a_w_in, a_conv_w, a_conv_b, a_gate_w, a_gate_b, a_lambda,
                         a_w_out, first=i == 0)
        else:
            x = _layer_b(x, j, i, norm_g, b_w_in, b_conv_w, b_w_out, fg, final=i == DEPTH - 1)
    return x
```

```python
import functools

import jax
import jax.numpy as jnp
from jax import lax
from jax.experimental import pallas as pl
from jax.experimental.pallas import tpu as pltpu

D_MODEL = 1024
BATCH = 8
DEPTH = 4
WIDTH = 3 * D_MODEL // 2
LRU_HEADS = 12
LRU_BLOCK = WIDTH // LRU_HEADS
CONV_A = 4
CONV_B = 3
LRU_C = 8.0
NORM_EPS = 1e-6

TILE_T = 64
TILE_M = TILE_T * BATCH
CHUNK = 256
HALO_A = (CONV_A - 1) * BATCH
HALO_B = (CONV_B - 1) * BATCH
BLOCK_ROWS = 64
NORM_ROWS = 32
GATE_K = 2 * LRU_BLOCK
PREP_SLABS = 4
F32_MIN_NORMAL = 1.1754943508222875e-38
LOG2_E = 1.4426950408889634
LN_2 = 0.6931471805599453
VMEM_LIMIT_BYTES = 56 * 1024 * 1024


def _dot(a, b):
    return lax.dot_general(a, b, (((1,), (0,)), ((), ())), preferred_element_type=jnp.float32)


def _rmsnorm(x, g):
    inv = lax.rsqrt(jnp.mean(x * x, axis=-1, keepdims=True) + NORM_EPS)
    return x * inv * g


def _silu_from_half(gh):
    return gh * jnp.tanh(gh) + gh


def _rmsnorm_rows(x_ref, g_ref, dst_ref):
    for r0 in range(0, TILE_M, NORM_ROWS):
        rows = slice(r0, r0 + NORM_ROWS)
        dst_ref[rows, :] = _rmsnorm(x_ref[rows, :], g_ref[...])


def _to_time_major(x_ref, dst_ref):
    for t0 in range(0, TILE_T, BATCH):
        for lo in range(0, D_MODEL, LRU_BLOCK):
            v = x_ref[:, t0:t0 + BATCH, lo:lo + LRU_BLOCK]
            dst_ref[t0 * BATCH:(t0 + BATCH) * BATCH, lo:lo + LRU_BLOCK] = (
                jnp.swapaxes(v, 0, 1).reshape(BATCH * BATCH, LRU_BLOCK))


def _store_batch_major(rows, o_ref, t0, lo):
    v = rows.reshape(BATCH, BATCH, rows.shape[-1])
    o_ref[:, t0:t0 + BATCH, lo:lo + rows.shape[-1]] = jnp.swapaxes(v, 0, 1)


def _prep_a_kernel(win_ref, wout_ref, gw_ref, gb_ref, win_o, wout_o, gw_o):
    bf16 = jnp.bfloat16
    win_o[:, :WIDTH] = (0.5 * win_ref[:, :WIDTH]).astype(bf16)
    win_o[:, WIDTH:] = win_ref[:, WIDTH:].astype(bf16)
    wout_o[...] = wout_ref[...].astype(bf16)
    heads = gw_ref.shape[0]
    pad_rows = GATE_K - LRU_BLOCK
    row = lax.broadcasted_iota(jnp.int32, (pad_rows, 2 * LRU_BLOCK), 0)
    for hh in range(heads):
        gw_o[hh, :LRU_BLOCK, :] = gw_ref[hh].astype(bf16)
        half_b = 0.5 * gb_ref[hh]
        hi = half_b.astype(bf16).astype(jnp.float32)
        lo = half_b - hi
        gw_o[hh, LRU_BLOCK:, :] = jnp.where(row == 0, hi, jnp.where(row == 1, lo, 0.0)).astype(bf16)


def _layer_a_kernel(x_ref, g_ref, win_ref, cw_ref, cb_ref, gw_ref, lam_ref, wout_ref, o_ref,
                    h_ref, gate_ref, xb_ref, uh_ref, a_ref, b_ref, hs_ref, rowc_ref, ones_ref, *xtm,
                    first):
    @pl.when(pl.program_id(0) == 0)
    def _():
        xb_ref[0:HALO_A, :] = jnp.zeros((HALO_A, WIDTH), jnp.float32)
        hs_ref[...] = jnp.zeros_like(hs_ref)
        nl = -lam_ref[...]
        softplus = jnp.maximum(nl, 0.0) + jnp.log1p(jnp.exp(-jnp.abs(nl)))
        rowc_ref[0:1, :] = (-0.5 * LRU_C * LOG2_E) * softplus
        rowc_ref[1:1 + CONV_A, :] = 0.5 * cw_ref[...]
        rowc_ref[1 + CONV_A:2 + CONV_A, :] = 0.5 * cb_ref[...]
        lane = lax.broadcasted_iota(jnp.int32, ones_ref.shape, 1)
        ones_ref[...] = jnp.where(lane < 2, 1.0, 0.0)

    if first:
        (x_rows_ref,) = xtm
        _to_time_major(x_ref, x_rows_ref)
    else:
        x_rows_ref = x_ref
    _rmsnorm_rows(x_rows_ref, g_ref, h_ref)

    def in_proj(c):
        lo = c * CHUNK
        gate_ref[:, lo:lo + CHUNK] = _dot(h_ref[...], win_ref[:, lo:lo + CHUNK])
        xb_ref[HALO_A:, lo:lo + CHUNK] = _dot(h_ref[...], win_ref[:, WIDTH + lo:WIDTH + lo + CHUNK])

    def conv(c):
        for hl in range(c * CHUNK, (c + 1) * CHUNK, LRU_BLOCK):
            cols = slice(hl, hl + LRU_BLOCK)
            for r0 in range(0, TILE_M, BLOCK_ROWS):
                win = xb_ref[r0:r0 + BLOCK_ROWS + HALO_A, cols]
                acc = win[0:BLOCK_ROWS] * rowc_ref[1:2, cols]
                for k in range(1, CONV_A):
                    acc = acc + win[k * BATCH:k * BATCH + BLOCK_ROWS] * rowc_ref[1 + k:2 + k, cols]
                uh_ref[r0:r0 + BLOCK_ROWS, cols] = acc + rowc_ref[1 + CONV_A:2 + CONV_A, cols]
            xb_ref[0:HALO_A, cols] = xb_ref[TILE_M:TILE_M + HALO_A, cols]

    def gate_proj(c):
        for hl in range(c * CHUNK, (c + 1) * CHUNK, LRU_BLOCK):
            cols = slice(hl, hl + LRU_BLOCK)
            lhs = jnp.concatenate([uh_ref[:, cols], ones_ref[...]], axis=1)
            g2 = _dot(lhs, gw_ref[hl // LRU_BLOCK])
            a_ref[:, cols] = g2[:, :LRU_BLOCK]
            b_ref[:, cols] = g2[:, LRU_BLOCK:]

    def gates(c):
        for hl in range(c * CHUNK, (c + 1) * CHUNK, LRU_BLOCK):
            cols = slice(hl, hl + LRU_BLOCK)
            for r0 in range(0, TILE_M, BLOCK_ROWS):
                rows = slice(r0, r0 + BLOCK_ROWS)
                t_r = jnp.tanh(a_ref[rows, cols])
                t_i = jnp.tanh(b_ref[rows, cols])
                k2 = rowc_ref[0:1, cols]
                log2_a = t_r * k2 + k2
                a = jnp.exp2(log2_a)
                m2 = jnp.tanh(log2_a * (-LN_2)) * (1.0 + a * a)
                mult = m2 * lax.rsqrt(jnp.maximum(m2, F32_MIN_NORMAL))
                uh = uh_ref[rows, cols]
                a_ref[rows, cols] = a
                b_ref[rows, cols] = mult * (t_i * uh + uh)
                gate_ref[rows, cols] = _silu_from_half(gate_ref[rows, cols])

    n_chunks = WIDTH // CHUNK
    in_proj(0)
    conv(0)
    gate_proj(0)
    for c in range(n_chunks):
        if c + 1 < n_chunks:
            in_proj(c + 1)
        gates(c)
        if c + 1 < n_chunks:
            conv(c + 1)
            gate_proj(c + 1)

    hs = hs_ref[...]
    for t in range(TILE_T):
        rows = slice(t * BATCH, (t + 1) * BATCH)
        hs = a_ref[rows, :] * hs + b_ref[rows, :]
        b_ref[rows, :] = hs * gate_ref[rows, :]
    hs_ref[...] = hs

    o_ref[...] = x_rows_ref[...] + _dot(b_ref[...], wout_ref[...])


def _layer_b_kernel(x_ref, g_ref, win_ref, cw_ref, wout_ref, fg_ref, o_ref, p_ref, z_ref, *rows_scratch,
                    final):
    @pl.when(pl.program_id(0) == 0)
    def _():
        p_ref[0:HALO_B, :] = jnp.zeros((HALO_B, WIDTH), jnp.float32)

    x = x_ref[...]
    h = _rmsnorm(x, g_ref[...])

    for c in range(WIDTH // CHUNK):
        lo = c * CHUNK
        bg = _dot(h, win_ref[:, lo:lo + CHUNK])
        cg = _dot(h, win_ref[:, WIDTH + lo:WIDTH + lo + CHUNK])
        xv = _dot(h, win_ref[:, 2 * WIDTH + lo:2 * WIDTH + lo + CHUNK])
        gate = _dot(h, win_ref[:, 3 * WIDTH + lo:3 * WIDTH + lo + CHUNK])
        p_ref[HALO_B:, lo:lo + CHUNK] = cg * xv
        conv = p_ref[0:TILE_M, lo:lo + CHUNK] * cw_ref[0:1, lo:lo + CHUNK]
        for k in range(1, CONV_B):
            conv = conv + p_ref[k * BATCH:k * BATCH + TILE_M, lo:lo + CHUNK] * cw_ref[k:k + 1, lo:lo + CHUNK]
        p_ref[0:HALO_B, lo:lo + CHUNK] = p_ref[TILE_M:TILE_M + HALO_B, lo:lo + CHUNK]
        z_ref[:, lo:lo + CHUNK] = (bg * conv) * _silu_from_half(0.5 * gate)

    out = x + _dot(z_ref[...], wout_ref[...])
    if not final:
        o_ref[...] = out
    else:
        (rows_ref,) = rows_scratch
        rows_ref[...] = out
        for t0 in range(0, TILE_T, BATCH):
            rows = slice(t0 * BATCH, (t0 + BATCH) * BATCH)
            v = rows_ref[rows, :]
            inv = lax.rsqrt(jnp.mean(v * v, axis=-1, keepdims=True) + NORM_EPS)
            for lo in range(0, D_MODEL, LRU_BLOCK):
                y = rows_ref[rows, lo:lo + LRU_BLOCK] * inv * fg_ref[:, lo:lo + LRU_BLOCK]
                _store_batch_major(y, o_ref, t0, lo)


def _resident(shape, *index):
    block = (None,) * len(index) + tuple(shape[len(index):])
    full_index = tuple(index) + (0,) * (len(shape) - len(index))
    return pl.BlockSpec(block, lambda i: full_index, pipeline_mode=pl.Buffered(1))


def _time_major_tile():
    return pl.BlockSpec((TILE_M, D_MODEL), lambda i: (i, 0))


def _batch_major_tile():
    return pl.BlockSpec((BATCH, TILE_T, D_MODEL), lambda i: (0, i, 0))


def _compiler_params(n_axes=1):
    return pltpu.CompilerParams(dimension_semantics=("arbitrary",) * n_axes,
                                vmem_limit_bytes=VMEM_LIMIT_BYTES)


def _wide_scratch():
    return pltpu.VMEM((TILE_M, WIDTH), jnp.float32)


def _prep_a(w_in, w_out, gw, gb):
    n_a = w_in.shape[0]
    heads = LRU_HEADS // PREP_SLABS

    def slab(rows, cols):
        return pl.BlockSpec((None, rows // PREP_SLABS, cols), lambda l, s: (l, s, 0))

    return pl.pallas_call(
        _prep_a_kernel,
        grid=(n_a, PREP_SLABS),
        in_specs=[slab(D_MODEL, 2 * WIDTH), slab(WIDTH, D_MODEL),
                  pl.BlockSpec((None, heads, LRU_BLOCK, 2 * LRU_BLOCK), lambda l, s: (l, s, 0, 0)),
                  pl.BlockSpec((None, heads, 1, 2 * LRU_BLOCK), lambda l, s: (l, s, 0, 0))],
        out_specs=[slab(D_MODEL, 2 * WIDTH), slab(WIDTH, D_MODEL),
                   pl.BlockSpec((None, heads, GATE_K, 2 * LRU_BLOCK), lambda l, s: (l, s, 0, 0))],
        out_shape=[jax.ShapeDtypeStruct(w_in.shape, jnp.bfloat16),
                   jax.ShapeDtypeStruct(w_out.shape, jnp.bfloat16),
                   jax.ShapeDtypeStruct((n_a, LRU_HEADS, GATE_K, 2 * LRU_BLOCK), jnp.bfloat16)],
        compiler_params=_compiler_params(2),
        name="rglru_weight_prep",
    )(w_in, w_out, gw, gb.reshape(n_a, LRU_HEADS, 1, 2 * LRU_BLOCK))


def _layer_a(x, j, layer, norm_g, w_in, cw, cb, gw, lam, w_out, first):
    seq = x.shape[1] if first else x.shape[0] // BATCH
    scratch = [pltpu.VMEM((TILE_M, D_MODEL), jnp.float32),
               _wide_scratch(),
               pltpu.VMEM((HALO_A + TILE_M, WIDTH), jnp.float32),
               _wide_scratch(),
               _wide_scratch(),
               _wide_scratch(),
               pltpu.VMEM((BATCH, WIDTH), jnp.float32),
               pltpu.VMEM((BATCH, WIDTH), jnp.float32),
               pltpu.VMEM((TILE_M, GATE_K - LRU_BLOCK), jnp.float32)]
    if first:
        scratch.append(pltpu.VMEM((TILE_M, D_MODEL), jnp.float32))
    return pl.pallas_call(
        functools.partial(_layer_a_kernel, first=first),
        grid=(seq // TILE_T,),
        in_specs=[_batch_major_tile() if first else _time_major_tile(),
                  _resident(norm_g.shape, layer), _resident(w_in.shape, j), _resident(cw.shape, j),
                  _resident(cb.shape, j), _resident(gw.shape, j),
                  _resident(lam.shape, j), _resident(w_out.shape, j)],
        out_specs=_time_major_tile(),
        out_shape=jax.ShapeDtypeStruct((seq * BATCH, D_MODEL), x.dtype),
        scratch_shapes=scratch,
        compiler_params=_compiler_params(),
        name="rglru_layer",
    )(x, norm_g, w_in, cw, cb, gw, lam, w_out)


def _layer_b(x, j, layer, norm_g, w_in, cw, w_out, fg, final):
    seq = x.shape[0] // BATCH
    out_shape = (BATCH, seq, D_MODEL) if final else x.shape
    scratch = [pltpu.VMEM((HALO_B + TILE_M, WIDTH), jnp.float32),
               _wide_scratch()]
    if final:
        scratch.append(pltpu.VMEM((TILE_M, D_MODEL), jnp.float32))
    return pl.pallas_call(
        functools.partial(_layer_b_kernel, final=final),
        grid=(seq // TILE_T,),
        in_specs=[_time_major_tile(), _resident(norm_g.shape, layer), _resident(w_in.shape, j),
                  _resident(cw.shape, j), _resident(w_out.shape, j), _resident(fg.shape)],
        out_specs=_batch_major_tile() if final else _time_major_tile(),
        out_shape=jax.ShapeDtypeStruct(out_shape, x.dtype),
        scratch_shapes=scratch,
        compiler_params=_compiler_params(),
        name="shortconv_layer",
    )(x, norm_g, w_in, cw, w_out, fg)


def kernel(x, norm_g, a_w_in, a_conv_w, a_conv_b, a_gate_w, a_gate_b, a_lambda, a_w_out,
           b_w_in, b_conv_w, b_w_out, final_g):
    bsz, seq, d = x.shape
    assert (bsz, d) == (BATCH, D_MODEL) and seq % TILE_T == 0 and DEPTH % 2 == 0
    n_a = a_w_in.shape[0]
    norm_g = norm_g.reshape(DEPTH, 1, d)
    a_conv_b = a_conv_b.reshape(n_a, 1, WIDTH)
    a_lambda = a_lambda.reshape(n_a, 1, WIDTH)
    fg = final_g.reshape(1, d)
    a_w_in16, a_w_out16, a_gate_w16 = _prep_a(a_w_in, a_w_out, a_gate_w, a_gate_b)
    for i in range(DEPTH):
        j = i // 2
        if i % 2 == 0:
            x = _layer_a(x, j, i, norm_g, a_w_in16, a_conv_w, a_conv_b, a_gate_w16, a_lambda,
                         a_w_out16, first=i == 0)
        else:
            x = _layer_b(x, j, i, norm_g, b_w_in, b_conv_w, b_w_out, fg, final=i == DEPTH - 1)
    return x
```

```python
import functools

import jax
import jax.numpy as jnp
from jax import lax
from jax.experimental import pallas as pl
from jax.experimental.pallas import tpu as pltpu

D_MODEL = 1024
BATCH = 8
DEPTH = 4
WIDTH = 3 * D_MODEL // 2
LRU_HEADS = 12
LRU_BLOCK = WIDTH // LRU_HEADS
CONV_A = 4
CONV_B = 3
LRU_C = 8.0
NORM_EPS = 1e-6

TILE_T_A = 128
TILE_T_B = 64
CHUNK = 256
HALO_A = (CONV_A - 1) * BATCH
HALO_B = (CONV_B - 1) * BATCH
BLOCK_ROWS = 64
NORM_ROWS = 32
GATE_K = 2 * LRU_BLOCK
PREP_SLABS = 4
F32_MIN_NORMAL = 1.1754943508222875e-38
LOG2_E = 1.4426950408889634
LN_2 = 0.6931471805599453
VMEM_LIMIT_BYTES = 56 * 1024 * 1024


def _dot(a, b):
    return lax.dot_general(a, b, (((1,), (0,)), ((), ())), preferred_element_type=jnp.float32)


def _rmsnorm(x, g):
    inv = lax.rsqrt(jnp.mean(x * x, axis=-1, keepdims=True) + NORM_EPS)
    return x * inv * g


def _silu_from_half(gh):
    return gh * jnp.tanh(gh) + gh


def _rmsnorm_rows(x_ref, g_ref, dst_ref):
    for r0 in range(0, dst_ref.shape[0], NORM_ROWS):
        rows = slice(r0, r0 + NORM_ROWS)
        dst_ref[rows, :] = _rmsnorm(x_ref[rows, :], g_ref[...])


def _to_time_major(x_ref, dst_ref):
    for t0 in range(0, x_ref.shape[1], BATCH):
        for lo in range(0, D_MODEL, LRU_BLOCK):
            v = x_ref[:, t0:t0 + BATCH, lo:lo + LRU_BLOCK]
            dst_ref[t0 * BATCH:(t0 + BATCH) * BATCH, lo:lo + LRU_BLOCK] = (
                jnp.swapaxes(v, 0, 1).reshape(BATCH * BATCH, LRU_BLOCK))


def _store_batch_major(rows, o_ref, t0, lo):
    v = rows.reshape(BATCH, BATCH, rows.shape[-1])
    o_ref[:, t0:t0 + BATCH, lo:lo + rows.shape[-1]] = jnp.swapaxes(v, 0, 1)


def _prep_a_kernel(win_ref, wout_ref, gw_ref, gb_ref, win_o, wout_o, gw_o):
    bf16 = jnp.bfloat16
    win_o[:, :WIDTH] = (0.5 * win_ref[:, :WIDTH]).astype(bf16)
    win_o[:, WIDTH:] = win_ref[:, WIDTH:].astype(bf16)
    wout_o[...] = wout_ref[...].astype(bf16)
    heads = gw_ref.shape[0]
    pad_rows = GATE_K - LRU_BLOCK
    row = lax.broadcasted_iota(jnp.int32, (pad_rows, 2 * LRU_BLOCK), 0)
    for hh in range(heads):
        gw_o[hh, :LRU_BLOCK, :] = gw_ref[hh].astype(bf16)
        half_b = 0.5 * gb_ref[hh]
        hi = half_b.astype(bf16).astype(jnp.float32)
        lo = half_b - hi
        gw_o[hh, LRU_BLOCK:, :] = jnp.where(row == 0, hi, jnp.where(row == 1, lo, 0.0)).astype(bf16)


def _layer_a_kernel(x_ref, g_ref, win_ref, cw_ref, cb_ref, gw_ref, lam_ref, wout_ref, o_ref,
                    h_ref, gate_ref, xb_ref, uh_ref, a_ref, b_ref, hs_ref, rowc_ref, ones_ref, *xtm,
                    first):
    tile_m = o_ref.shape[0]
    @pl.when(pl.program_id(0) == 0)
    def _():
        xb_ref[0:HALO_A, :] = jnp.zeros((HALO_A, WIDTH), jnp.float32)
        hs_ref[...] = jnp.zeros_like(hs_ref)
        nl = -lam_ref[...]
        softplus = jnp.maximum(nl, 0.0) + jnp.log1p(jnp.exp(-jnp.abs(nl)))
        rowc_ref[0:1, :] = (-0.5 * LRU_C * LOG2_E) * softplus
        rowc_ref[1:1 + CONV_A, :] = 0.5 * cw_ref[...]
        rowc_ref[1 + CONV_A:2 + CONV_A, :] = 0.5 * cb_ref[...]
        lane = lax.broadcasted_iota(jnp.int32, ones_ref.shape, 1)
        ones_ref[...] = jnp.where(lane < 2, 1.0, 0.0)

    if first:
        (x_rows_ref,) = xtm
        _to_time_major(x_ref, x_rows_ref)
    else:
        x_rows_ref = x_ref
    _rmsnorm_rows(x_rows_ref, g_ref, h_ref)

    def in_proj(c):
        lo = c * CHUNK
        gate_ref[:, lo:lo + CHUNK] = _dot(h_ref[...], win_ref[:, lo:lo + CHUNK])
        xb_ref[HALO_A:, lo:lo + CHUNK] = _dot(h_ref[...], win_ref[:, WIDTH + lo:WIDTH + lo + CHUNK])

    def conv(c):
        for hl in range(c * CHUNK, (c + 1) * CHUNK, LRU_BLOCK):
            cols = slice(hl, hl + LRU_BLOCK)
            for r0 in range(0, tile_m, BLOCK_ROWS):
                win = xb_ref[r0:r0 + BLOCK_ROWS + HALO_A, cols]
                acc = win[0:BLOCK_ROWS] * rowc_ref[1:2, cols]
                for k in range(1, CONV_A):
                    acc = acc + win[k * BATCH:k * BATCH + BLOCK_ROWS] * rowc_ref[1 + k:2 + k, cols]
                uh_ref[r0:r0 + BLOCK_ROWS, cols] = acc + rowc_ref[1 + CONV_A:2 + CONV_A, cols]
            xb_ref[0:HALO_A, cols] = xb_ref[tile_m:tile_m + HALO_A, cols]

    def gate_proj(c):
        for hl in range(c * CHUNK, (c + 1) * CHUNK, LRU_BLOCK):
            cols = slice(hl, hl + LRU_BLOCK)
            lhs = jnp.concatenate([uh_ref[:, cols], ones_ref[...]], axis=1)
            g2 = _dot(lhs, gw_ref[hl // LRU_BLOCK])
            a_ref[:, cols] = g2[:, :LRU_BLOCK]
            b_ref[:, cols] = g2[:, LRU_BLOCK:]

    def gates(c):
        for hl in range(c * CHUNK, (c + 1) * CHUNK, LRU_BLOCK):
            cols = slice(hl, hl + LRU_BLOCK)
            for r0 in range(0, tile_m, BLOCK_ROWS):
                rows = slice(r0, r0 + BLOCK_ROWS)
                t_r = jnp.tanh(a_ref[rows, cols])
                t_i = jnp.tanh(b_ref[rows, cols])
                k2 = rowc_ref[0:1, cols]
                log2_a = t_r * k2 + k2
                a = jnp.exp2(log2_a)
                m2 = jnp.tanh(log2_a * (-LN_2)) * (1.0 + a * a)
                mult = m2 * lax.rsqrt(jnp.maximum(m2, F32_MIN_NORMAL))
                uh = uh_ref[rows, cols]
                a_ref[rows, cols] = a
                b_ref[rows, cols] = mult * (t_i * uh + uh)
                gate_ref[rows, cols] = _silu_from_half(gate_ref[rows, cols])

    n_chunks = WIDTH // CHUNK
    in_proj(0)
    conv(0)
    gate_proj(0)
    for c in range(n_chunks):
        if c + 1 < n_chunks:
            in_proj(c + 1)
        gates(c)
        if c + 1 < n_chunks:
            conv(c + 1)
            gate_proj(c + 1)

    hs = hs_ref[...]
    for t in range(tile_m // BATCH):
        rows = slice(t * BATCH, (t + 1) * BATCH)
        hs = a_ref[rows, :] * hs + b_ref[rows, :]
        b_ref[rows, :] = hs * gate_ref[rows, :]
    hs_ref[...] = hs

    o_ref[...] = x_rows_ref[...] + _dot(b_ref[...], wout_ref[...])


def _layer_b_kernel(x_ref, g_ref, win_ref, cw_ref, wout_ref, fg_ref, o_ref, p_ref, z_ref, *rows_scratch,
                    final):
    tile_m = x_ref.shape[0]

    @pl.when(pl.program_id(0) == 0)
    def _():
        p_ref[0:HALO_B, :] = jnp.zeros((HALO_B, WIDTH), jnp.float32)

    x = x_ref[...]
    h = _rmsnorm(x, g_ref[...])

    for c in range(WIDTH // CHUNK):
        lo = c * CHUNK
        bg = _dot(h, win_ref[:, lo:lo + CHUNK])
        cg = _dot(h, win_ref[:, WIDTH + lo:WIDTH + lo + CHUNK])
        xv = _dot(h, win_ref[:, 2 * WIDTH + lo:2 * WIDTH + lo + CHUNK])
        gate = _dot(h, win_ref[:, 3 * WIDTH + lo:3 * WIDTH + lo + CHUNK])
        p_ref[HALO_B:, lo:lo + CHUNK] = cg * xv
        conv = p_ref[0:tile_m, lo:lo + CHUNK] * cw_ref[0:1, lo:lo + CHUNK]
        for k in range(1, CONV_B):
            conv = conv + p_ref[k * BATCH:k * BATCH + tile_m, lo:lo + CHUNK] * cw_ref[k:k + 1, lo:lo + CHUNK]
        p_ref[0:HALO_B, lo:lo + CHUNK] = p_ref[tile_m:tile_m + HALO_B, lo:lo + CHUNK]
        z_ref[:, lo:lo + CHUNK] = (bg * conv) * _silu_from_half(0.5 * gate)

    out = x + _dot(z_ref[...], wout_ref[...])
    if not final:
        o_ref[...] = out
    else:
        (rows_ref,) = rows_scratch
        rows_ref[...] = out
        for t0 in range(0, tile_m // BATCH, BATCH):
            rows = slice(t0 * BATCH, (t0 + BATCH) * BATCH)
            v = rows_ref[rows, :]
            inv = lax.rsqrt(jnp.mean(v * v, axis=-1, keepdims=True) + NORM_EPS)
            for lo in range(0, D_MODEL, LRU_BLOCK):
                y = rows_ref[rows, lo:lo + LRU_BLOCK] * inv * fg_ref[:, lo:lo + LRU_BLOCK]
                _store_batch_major(y, o_ref, t0, lo)


def _resident(shape, *index):
    block = (None,) * len(index) + tuple(shape[len(index):])
    full_index = tuple(index) + (0,) * (len(shape) - len(index))
    return pl.BlockSpec(block, lambda i: full_index, pipeline_mode=pl.Buffered(1))


def _time_major_tile(tile_t):
    return pl.BlockSpec((tile_t * BATCH, D_MODEL), lambda i: (i, 0))


def _batch_major_tile(tile_t):
    return pl.BlockSpec((BATCH, tile_t, D_MODEL), lambda i: (0, i, 0))


def _compiler_params(n_axes=1):
    return pltpu.CompilerParams(dimension_semantics=("arbitrary",) * n_axes,
                                vmem_limit_bytes=VMEM_LIMIT_BYTES)


def _wide_scratch(tile_m):
    return pltpu.VMEM((tile_m, WIDTH), jnp.float32)


def _prep_a(w_in, w_out, gw, gb):
    n_a = w_in.shape[0]
    heads = LRU_HEADS // PREP_SLABS

    def slab(rows, cols):
        return pl.BlockSpec((None, rows // PREP_SLABS, cols), lambda l, s: (l, s, 0))

    return pl.pallas_call(
        _prep_a_kernel,
        grid=(n_a, PREP_SLABS),
        in_specs=[slab(D_MODEL, 2 * WIDTH), slab(WIDTH, D_MODEL),
                  pl.BlockSpec((None, heads, LRU_BLOCK, 2 * LRU_BLOCK), lambda l, s: (l, s, 0, 0)),
                  pl.BlockSpec((None, heads, 1, 2 * LRU_BLOCK), lambda l, s: (l, s, 0, 0))],
        out_specs=[slab(D_MODEL, 2 * WIDTH), slab(WIDTH, D_MODEL),
                   pl.BlockSpec((None, heads, GATE_K, 2 * LRU_BLOCK), lambda l, s: (l, s, 0, 0))],
        out_shape=[jax.ShapeDtypeStruct(w_in.shape, jnp.bfloat16),
                   jax.ShapeDtypeStruct(w_out.shape, jnp.bfloat16),
                   jax.ShapeDtypeStruct((n_a, LRU_HEADS, GATE_K, 2 * LRU_BLOCK), jnp.bfloat16)],
        compiler_params=_compiler_params(2),
        name="rglru_weight_prep",
    )(w_in, w_out, gw, gb.reshape(n_a, LRU_HEADS, 1, 2 * LRU_BLOCK))


def _layer_a(x, j, layer, norm_g, w_in, cw, cb, gw, lam, w_out, first):
    seq = x.shape[1] if first else x.shape[0] // BATCH
    tile_t, tile_m = TILE_T_A, TILE_T_A * BATCH
    scratch = [pltpu.VMEM((tile_m, D_MODEL), jnp.float32),
               _wide_scratch(tile_m),
               pltpu.VMEM((HALO_A + tile_m, WIDTH), jnp.float32),
               _wide_scratch(tile_m),
               _wide_scratch(tile_m),
               _wide_scratch(tile_m),
               pltpu.VMEM((BATCH, WIDTH), jnp.float32),
               pltpu.VMEM((BATCH, WIDTH), jnp.float32),
               pltpu.VMEM((tile_m, GATE_K - LRU_BLOCK), jnp.float32)]
    if first:
        scratch.append(pltpu.VMEM((tile_m, D_MODEL), jnp.float32))
    return pl.pallas_call(
        functools.partial(_layer_a_kernel, first=first),
        grid=(seq // tile_t,),
        in_specs=[_batch_major_tile(tile_t) if first else _time_major_tile(tile_t),
                  _resident(norm_g.shape, layer), _resident(w_in.shape, j), _resident(cw.shape, j),
                  _resident(cb.shape, j), _resident(gw.shape, j),
                  _resident(lam.shape, j), _resident(w_out.shape, j)],
        out_specs=_time_major_tile(tile_t),
        out_shape=jax.ShapeDtypeStruct((seq * BATCH, D_MODEL), x.dtype),
        scratch_shapes=scratch,
        compiler_params=_compiler_params(),
        name="rglru_layer",
    )(x, norm_g, w_in, cw, cb, gw, lam, w_out)


def _layer_b(x, j, layer, norm_g, w_in, cw, w_out, fg, final):
    seq = x.shape[0] // BATCH
    tile_t, tile_m = TILE_T_B, TILE_T_B * BATCH
    out_shape = (BATCH, seq, D_MODEL) if final else x.shape
    scratch = [pltpu.VMEM((HALO_B + tile_m, WIDTH), jnp.float32),
               _wide_scratch(tile_m)]
    if final:
        scratch.append(pltpu.VMEM((tile_m, D_MODEL), jnp.float32))
    return pl.pallas_call(
        functools.partial(_layer_b_kernel, final=final),
        grid=(seq // tile_t,),
        in_specs=[_time_major_tile(tile_t), _resident(norm_g.shape, layer), _resident(w_in.shape, j),
                  _resident(cw.shape, j), _resident(w_out.shape, j), _resident(fg.shape)],
        out_specs=_batch_major_tile(tile_t) if final else _time_major_tile(tile_t),
        out_shape=jax.ShapeDtypeStruct(out_shape, x.dtype),
        scratch_shapes=scratch,
        compiler_params=_compiler_params(),
        name="shortconv_layer",
    )(x, norm_g, w_in, cw, w_out, fg)


def kernel(x, norm_g, a_w_in, a_conv_w, a_conv_b, a_gate_w, a_gate_b, a_lambda, a_w_out,
           b_w_in, b_conv_w, b_w_out, final_g):
    bsz, seq, d = x.shape
    assert (bsz, d) == (BATCH, D_MODEL) and seq % TILE_T_A == 0 and seq % TILE_T_B == 0 and DEPTH % 2 == 0
    n_a = a_w_in.shape[0]
    norm_g = norm_g.reshape(DEPTH, 1, d)
    a_conv_b = a_conv_b.reshape(n_a, 1, WIDTH)
    a_lambda = a_lambda.reshape(n_a, 1, WIDTH)
    fg = final_g.reshape(1, d)
    a_w_in16, a_w_out16, a_gate_w16 = _prep_a(a_w_in, a_w_out, a_gate_w, a_gate_b)
    for i in range(DEPTH):
        j = i // 2
        if i % 2 == 0:
            x = _layer_a(x, j, i, norm_g, a_w_in16, a_conv_w, a_conv_b, a_gate_w16, a_lambda,
                         a_w_out16, first=i == 0)
        else:
            x = _layer_b(x, j, i, norm_g, b_w_in, b_conv_w, b_w_out, fg, final=i == DEPTH - 1)
    return x
```

```python
import functools

import jax
import jax.numpy as jnp
from jax import lax
from jax.experimental import pallas as pl
from jax.experimental.pallas import tpu as pltpu

D_MODEL = 1024
BATCH = 8
DEPTH = 4
WIDTH = 3 * D_MODEL // 2
LRU_HEADS = 12
LRU_BLOCK = WIDTH // LRU_HEADS
CONV_A = 4
CONV_B = 3
LRU_C = 8.0
NORM_EPS = 1e-6

TILE_T_A = 128
TILE_T_B = 128
SUB_T_B = 64
CHUNK = 256
HALO_A = (CONV_A - 1) * BATCH
HALO_B = (CONV_B - 1) * BATCH
BLOCK_ROWS = 64
NORM_ROWS = 32
GATE_K = 2 * LRU_BLOCK
PREP_SLABS = 4
F32_MIN_NORMAL = 1.1754943508222875e-38
LOG2_E = 1.4426950408889634
LN_2 = 0.6931471805599453
VMEM_LIMIT_BYTES = 56 * 1024 * 1024


def _dot(a, b):
    return lax.dot_general(a, b, (((1,), (0,)), ((), ())), preferred_element_type=jnp.float32)


def _rmsnorm(x, g):
    inv = lax.rsqrt(jnp.mean(x * x, axis=-1, keepdims=True) + NORM_EPS)
    return x * inv * g


def _silu_from_half(gh):
    return gh * jnp.tanh(gh) + gh


def _rmsnorm_rows(x_ref, g_ref, dst_ref):
    for r0 in range(0, dst_ref.shape[0], NORM_ROWS):
        rows = slice(r0, r0 + NORM_ROWS)
        dst_ref[rows, :] = _rmsnorm(x_ref[rows, :], g_ref[...])


def _to_time_major(x_ref, dst_ref):
    for t0 in range(0, x_ref.shape[1], BATCH):
        for lo in range(0, D_MODEL, LRU_BLOCK):
            v = x_ref[:, t0:t0 + BATCH, lo:lo + LRU_BLOCK]
            dst_ref[t0 * BATCH:(t0 + BATCH) * BATCH, lo:lo + LRU_BLOCK] = (
                jnp.swapaxes(v, 0, 1).reshape(BATCH * BATCH, LRU_BLOCK))


def _store_batch_major(rows, o_ref, t0, lo):
    v = rows.reshape(BATCH, BATCH, rows.shape[-1])
    o_ref[:, pl.ds(t0, BATCH), lo:lo + rows.shape[-1]] = jnp.swapaxes(v, 0, 1)


def _prep_a_kernel(win_ref, wout_ref, gw_ref, gb_ref, win_o, wout_o, gw_o):
    bf16 = jnp.bfloat16
    win_o[:, :WIDTH] = (0.5 * win_ref[:, :WIDTH]).astype(bf16)
    win_o[:, WIDTH:] = win_ref[:, WIDTH:].astype(bf16)
    wout_o[...] = wout_ref[...].astype(bf16)
    heads = gw_ref.shape[0]
    pad_rows = GATE_K - LRU_BLOCK
    row = lax.broadcasted_iota(jnp.int32, (pad_rows, 2 * LRU_BLOCK), 0)
    for hh in range(heads):
        gw_o[hh, :LRU_BLOCK, :] = gw_ref[hh].astype(bf16)
        half_b = 0.5 * gb_ref[hh]
        hi = half_b.astype(bf16).astype(jnp.float32)
        lo = half_b - hi
        gw_o[hh, LRU_BLOCK:, :] = jnp.where(row == 0, hi, jnp.where(row == 1, lo, 0.0)).astype(bf16)


def _layer_a_kernel(x_ref, g_ref, win_ref, cw_ref, cb_ref, gw_ref, lam_ref, wout_ref, o_ref,
                    h_ref, gate_ref, xb_ref, uh_ref, a_ref, b_ref, hs_ref, rowc_ref, ones_ref, *xtm,
                    first):
    tile_m = o_ref.shape[0]
    @pl.when(pl.program_id(0) == 0)
    def _():
        xb_ref[0:HALO_A, :] = jnp.zeros((HALO_A, WIDTH), jnp.float32)
        hs_ref[...] = jnp.zeros_like(hs_ref)
        nl = -lam_ref[...]
        softplus = jnp.maximum(nl, 0.0) + jnp.log1p(jnp.exp(-jnp.abs(nl)))
        rowc_ref[0:1, :] = (-0.5 * LRU_C * LOG2_E) * softplus
        rowc_ref[1:1 + CONV_A, :] = 0.5 * cw_ref[...]
        rowc_ref[1 + CONV_A:2 + CONV_A, :] = 0.5 * cb_ref[...]
        lane = lax.broadcasted_iota(jnp.int32, ones_ref.shape, 1)
        ones_ref[...] = jnp.where(lane < 2, 1.0, 0.0)

    if first:
        (x_rows_ref,) = xtm
        _to_time_major(x_ref, x_rows_ref)
    else:
        x_rows_ref = x_ref
    _rmsnorm_rows(x_rows_ref, g_ref, h_ref)

    def in_proj(c):
        lo = c * CHUNK
        gate_ref[:, lo:lo + CHUNK] = _dot(h_ref[...], win_ref[:, lo:lo + CHUNK])
        xb_ref[HALO_A:, lo:lo + CHUNK] = _dot(h_ref[...], win_ref[:, WIDTH + lo:WIDTH + lo + CHUNK])

    def conv(c):
        for hl in range(c * CHUNK, (c + 1) * CHUNK, LRU_BLOCK):
            cols = slice(hl, hl + LRU_BLOCK)
            for r0 in range(0, tile_m, BLOCK_ROWS):
                win = xb_ref[r0:r0 + BLOCK_ROWS + HALO_A, cols]
                acc = win[0:BLOCK_ROWS] * rowc_ref[1:2, cols]
                for k in range(1, CONV_A):
                    acc = acc + win[k * BATCH:k * BATCH + BLOCK_ROWS] * rowc_ref[1 + k:2 + k, cols]
                uh_ref[r0:r0 + BLOCK_ROWS, cols] = acc + rowc_ref[1 + CONV_A:2 + CONV_A, cols]
            xb_ref[0:HALO_A, cols] = xb_ref[tile_m:tile_m + HALO_A, cols]

    def gate_proj(c):
        for hl in range(c * CHUNK, (c + 1) * CHUNK, LRU_BLOCK):
            cols = slice(hl, hl + LRU_BLOCK)
            lhs = jnp.concatenate([uh_ref[:, cols], ones_ref[...]], axis=1)
            g2 = _dot(lhs, gw_ref[hl // LRU_BLOCK])
            a_ref[:, cols] = g2[:, :LRU_BLOCK]
            b_ref[:, cols] = g2[:, LRU_BLOCK:]

    def gates(c):
        for hl in range(c * CHUNK, (c + 1) * CHUNK, LRU_BLOCK):
            cols = slice(hl, hl + LRU_BLOCK)
            for r0 in range(0, tile_m, BLOCK_ROWS):
                rows = slice(r0, r0 + BLOCK_ROWS)
                t_r = jnp.tanh(a_ref[rows, cols])
                t_i = jnp.tanh(b_ref[rows, cols])
                k2 = rowc_ref[0:1, cols]
                log2_a = t_r * k2 + k2
                a = jnp.exp2(log2_a)
                m2 = jnp.tanh(log2_a * (-LN_2)) * (1.0 + a * a)
                mult = m2 * lax.rsqrt(jnp.maximum(m2, F32_MIN_NORMAL))
                uh = uh_ref[rows, cols]
                a_ref[rows, cols] = a
                b_ref[rows, cols] = mult * (t_i * uh + uh)
                gate_ref[rows, cols] = _silu_from_half(gate_ref[rows, cols])

    n_chunks = WIDTH // CHUNK
    in_proj(0)
    conv(0)
    gate_proj(0)
    for c in range(n_chunks):
        if c + 1 < n_chunks:
            in_proj(c + 1)
        gates(c)
        if c + 1 < n_chunks:
            conv(c + 1)
            gate_proj(c + 1)

    hs = hs_ref[...]
    for t in range(tile_m // BATCH):
        rows = slice(t * BATCH, (t + 1) * BATCH)
        hs = a_ref[rows, :] * hs + b_ref[rows, :]
        b_ref[rows, :] = hs * gate_ref[rows, :]
    hs_ref[...] = hs

    o_ref[...] = x_rows_ref[...] + _dot(b_ref[...], wout_ref[...])


def _layer_b_kernel(x_ref, g_ref, win_ref, cw_ref, wout_ref, fg_ref, o_ref, p_ref, z_ref, *rows_scratch,
                    final):
    sub_m = z_ref.shape[0]

    @pl.when(pl.program_id(0) == 0)
    def _():
        p_ref[0:HALO_B, :] = jnp.zeros((HALO_B, WIDTH), jnp.float32)

    def mix_rows(s, carry):
        r0 = pl.multiple_of(s * sub_m, sub_m)
        x = x_ref[pl.ds(r0, sub_m), :]
        h = _rmsnorm(x, g_ref[...])

        for c in range(WIDTH // CHUNK):
            lo = c * CHUNK
            bg = _dot(h, win_ref[:, lo:lo + CHUNK])
            cg = _dot(h, win_ref[:, WIDTH + lo:WIDTH + lo + CHUNK])
            xv = _dot(h, win_ref[:, 2 * WIDTH + lo:2 * WIDTH + lo + CHUNK])
            gate = _dot(h, win_ref[:, 3 * WIDTH + lo:3 * WIDTH + lo + CHUNK])
            p_ref[HALO_B:, lo:lo + CHUNK] = cg * xv
            conv = p_ref[0:sub_m, lo:lo + CHUNK] * cw_ref[0:1, lo:lo + CHUNK]
            for k in range(1, CONV_B):
                conv = conv + p_ref[k * BATCH:k * BATCH + sub_m, lo:lo + CHUNK] * cw_ref[k:k + 1, lo:lo + CHUNK]
            p_ref[0:HALO_B, lo:lo + CHUNK] = p_ref[sub_m:sub_m + HALO_B, lo:lo + CHUNK]
            z_ref[:, lo:lo + CHUNK] = (bg * conv) * _silu_from_half(0.5 * gate)

        out = x + _dot(z_ref[...], wout_ref[...])
        if not final:
            o_ref[pl.ds(r0, sub_m), :] = out
        else:
            (rows_ref,) = rows_scratch
            rows_ref[...] = out
            t_base = pl.multiple_of(s * (sub_m // BATCH), BATCH)
            for t0 in range(0, sub_m // BATCH, BATCH):
                rows = slice(t0 * BATCH, (t0 + BATCH) * BATCH)
                v = rows_ref[rows, :]
                inv = lax.rsqrt(jnp.mean(v * v, axis=-1, keepdims=True) + NORM_EPS)
                for lo in range(0, D_MODEL, LRU_BLOCK):
                    y = rows_ref[rows, lo:lo + LRU_BLOCK] * inv * fg_ref[:, lo:lo + LRU_BLOCK]
                    _store_batch_major(y, o_ref, t_base + t0, lo)
        return carry

    lax.fori_loop(0, x_ref.shape[0] // sub_m, mix_rows, None)


def _resident(shape, *index):
    block = (None,) * len(index) + tuple(shape[len(index):])
    full_index = tuple(index) + (0,) * (len(shape) - len(index))
    return pl.BlockSpec(block, lambda i: full_index, pipeline_mode=pl.Buffered(1))


def _time_major_tile(tile_t):
    return pl.BlockSpec((tile_t * BATCH, D_MODEL), lambda i: (i, 0))


def _batch_major_tile(tile_t):
    return pl.BlockSpec((BATCH, tile_t, D_MODEL), lambda i: (0, i, 0))


def _compiler_params(n_axes=1):
    return pltpu.CompilerParams(dimension_semantics=("arbitrary",) * n_axes,
                                vmem_limit_bytes=VMEM_LIMIT_BYTES)


def _wide_scratch(tile_m):
    return pltpu.VMEM((tile_m, WIDTH), jnp.float32)


def _prep_a(w_in, w_out, gw, gb):
    n_a = w_in.shape[0]
    heads = LRU_HEADS // PREP_SLABS

    def slab(rows, cols):
        return pl.BlockSpec((None, rows // PREP_SLABS, cols), lambda l, s: (l, s, 0))

    return pl.pallas_call(
        _prep_a_kernel,
        grid=(n_a, PREP_SLABS),
        in_specs=[slab(D_MODEL, 2 * WIDTH), slab(WIDTH, D_MODEL),
                  pl.BlockSpec((None, heads, LRU_BLOCK, 2 * LRU_BLOCK), lambda l, s: (l, s, 0, 0)),
                  pl.BlockSpec((None, heads, 1, 2 * LRU_BLOCK), lambda l, s: (l, s, 0, 0))],
        out_specs=[slab(D_MODEL, 2 * WIDTH), slab(WIDTH, D_MODEL),
                   pl.BlockSpec((None, heads, GATE_K, 2 * LRU_BLOCK), lambda l, s: (l, s, 0, 0))],
        out_shape=[jax.ShapeDtypeStruct(w_in.shape, jnp.bfloat16),
                   jax.ShapeDtypeStruct(w_out.shape, jnp.bfloat16),
                   jax.ShapeDtypeStruct((n_a, LRU_HEADS, GATE_K, 2 * LRU_BLOCK), jnp.bfloat16)],
        compiler_params=_compiler_params(2),
        name="rglru_weight_prep",
    )(w_in, w_out, gw, gb.reshape(n_a, LRU_HEADS, 1, 2 * LRU_BLOCK))


def _layer_a(x, j, layer, norm_g, w_in, cw, cb, gw, lam, w_out, first):
    seq = x.shape[1] if first else x.shape[0] // BATCH
    tile_t, tile_m = TILE_T_A, TILE_T_A * BATCH
    scratch = [pltpu.VMEM((tile_m, D_MODEL), jnp.float32),
               _wide_scratch(tile_m),
               pltpu.VMEM((HALO_A + tile_m, WIDTH), jnp.float32),
               _wide_scratch(tile_m),
               _wide_scratch(tile_m),
               _wide_scratch(tile_m),
               pltpu.VMEM((BATCH, WIDTH), jnp.float32),
               pltpu.VMEM((BATCH, WIDTH), jnp.float32),
               pltpu.VMEM((tile_m, GATE_K - LRU_BLOCK), jnp.float32)]
    if first:
        scratch.append(pltpu.VMEM((tile_m, D_MODEL), jnp.float32))
    return pl.pallas_call(
        functools.partial(_layer_a_kernel, first=first),
        grid=(seq // tile_t,),
        in_specs=[_batch_major_tile(tile_t) if first else _time_major_tile(tile_t),
                  _resident(norm_g.shape, layer), _resident(w_in.shape, j), _resident(cw.shape, j),
                  _resident(cb.shape, j), _resident(gw.shape, j),
                  _resident(lam.shape, j), _resident(w_out.shape, j)],
        out_specs=_time_major_tile(tile_t),
        out_shape=jax.ShapeDtypeStruct((seq * BATCH, D_MODEL), x.dtype),
        scratch_shapes=scratch,
        compiler_params=_compiler_params(),
        name="rglru_layer",
    )(x, norm_g, w_in, cw, cb, gw, lam, w_out)


def _layer_b(x, j, layer, norm_g, w_in, cw, w_out, fg, final):
    seq = x.shape[0] // BATCH
    tile_t, tile_m = TILE_T_B, TILE_T_B * BATCH
    out_shape = (BATCH, seq, D_MODEL) if final else x.shape
    sub_m = SUB_T_B * BATCH
    scratch = [pltpu.VMEM((HALO_B + sub_m, WIDTH), jnp.float32),
               _wide_scratch(sub_m)]
    if final:
        scratch.append(pltpu.VMEM((sub_m, D_MODEL), jnp.float32))
    return pl.pallas_call(
        functools.partial(_layer_b_kernel, final=final),
        grid=(seq // tile_t,),
        in_specs=[_time_major_tile(tile_t), _resident(norm_g.shape, layer), _resident(w_in.shape, j),
                  _resident(cw.shape, j), _resident(w_out.shape, j), _resident(fg.shape)],
        out_specs=_batch_major_tile(tile_t) if final else _time_major_tile(tile_t),
        out_shape=jax.ShapeDtypeStruct(out_shape, x.dtype),
        scratch_shapes=scratch,
        compiler_params=_compiler_params(),
        name="shortconv_layer",
    )(x, norm_g, w_in, cw, w_out, fg)


def kernel(x, norm_g, a_w_in, a_conv_w, a_conv_b, a_gate_w, a_gate_b, a_lambda, a_w_out,
           b_w_in, b_conv_w, b_w_out, final_g):
    bsz, seq, d = x.shape
    assert (bsz, d) == (BATCH, D_MODEL) and seq % TILE_T_A == 0 and seq % TILE_T_B == 0 and TILE_T_B % SUB_T_B == 0 and DEPTH % 2 == 0
    n_a = a_w_in.shape[0]
    norm_g = norm_g.reshape(DEPTH, 1, d)
    a_conv_b = a_conv_b.reshape(n_a, 1, WIDTH)
    a_lambda = a_lambda.reshape(n_a, 1, WIDTH)
    fg = final_g.reshape(1, d)
    a_w_in16, a_w_out16, a_gate_w16 = _prep_a(a_w_in, a_w_out, a_gate_w, a_gate_b)
    for i in range(DEPTH):
        j = i // 2
        if i % 2 == 0:
            x = _layer_a(x, j, i, norm_g, a_w_in16, a_conv_w, a_conv_b, a_gate_w16, a_lambda,
                         a_w_out16, first=i == 0)
        else:
            x = _layer_b(x, j, i, norm_g, b_w_in, b_conv_w, b_w_out, fg, final=i == DEPTH - 1)
    return x
```

```python
import functools

import jax
import jax.numpy as jnp
from jax import lax
from jax.experimental import pallas as pl
from jax.experimental.pallas import tpu as pltpu

D_MODEL = 1024
BATCH = 8
DEPTH = 4
WIDTH = 3 * D_MODEL // 2
LRU_HEADS = 12
LRU_BLOCK = WIDTH // LRU_HEADS
CONV_A = 4
CONV_B = 3
LRU_C = 8.0
NORM_EPS = 1e-6

TILE_T_A = 128
TILE_T_B = 128
SUB_T_B = 64
CHUNK = 256
HALO_A = (CONV_A - 1) * BATCH
HALO_B = (CONV_B - 1) * BATCH
BLOCK_ROWS = 64
NORM_ROWS = 32
GATE_K = 2 * LRU_BLOCK
PREP_SLABS = 4
F32_MIN_NORMAL = 1.1754943508222875e-38
LOG2_E = 1.4426950408889634
LN_2 = 0.6931471805599453
VMEM_LIMIT_BYTES = 56 * 1024 * 1024


def _dot(a, b):
    return lax.dot_general(a, b, (((1,), (0,)), ((), ())), preferred_element_type=jnp.float32)


def _rmsnorm(x, g):
    inv = lax.rsqrt(jnp.mean(x * x, axis=-1, keepdims=True) + NORM_EPS)
    return x * inv * g


def _silu_from_half(gh):
    return gh * jnp.tanh(gh) + gh


def _rmsnorm_rows(x_ref, g_ref, dst_ref):
    for r0 in range(0, dst_ref.shape[0], NORM_ROWS):
        rows = slice(r0, r0 + NORM_ROWS)
        dst_ref[rows, :] = _rmsnorm(x_ref[rows, :], g_ref[...])


def _tile_fetches(x_hbm_ref, x_tiles_ref, sems, tile, slot):
    tile_t = x_tiles_ref.shape[1]
    return [pltpu.make_async_copy(x_hbm_ref.at[b, pl.ds(tile * tile_t, tile_t), :],
                                  x_tiles_ref.at[slot, :, b, :], sems.at[slot, b])
            for b in range(BATCH)]


def _store_batch_major(rows, o_ref, t0, lo):
    v = rows.reshape(BATCH, BATCH, rows.shape[-1])
    o_ref[:, pl.ds(t0, BATCH), lo:lo + rows.shape[-1]] = jnp.swapaxes(v, 0, 1)


def _prep_a_kernel(win_ref, wout_ref, gw_ref, gb_ref, win_o, wout_o, gw_o):
    bf16 = jnp.bfloat16
    win_o[:, :WIDTH] = (0.5 * win_ref[:, :WIDTH]).astype(bf16)
    win_o[:, WIDTH:] = win_ref[:, WIDTH:].astype(bf16)
    wout_o[...] = wout_ref[...].astype(bf16)
    heads = gw_ref.shape[0]
    pad_rows = GATE_K - LRU_BLOCK
    row = lax.broadcasted_iota(jnp.int32, (pad_rows, 2 * LRU_BLOCK), 0)
    for hh in range(heads):
        gw_o[hh, :LRU_BLOCK, :] = gw_ref[hh].astype(bf16)
        half_b = 0.5 * gb_ref[hh]
        hi = half_b.astype(bf16).astype(jnp.float32)
        lo = half_b - hi
        gw_o[hh, LRU_BLOCK:, :] = jnp.where(row == 0, hi, jnp.where(row == 1, lo, 0.0)).astype(bf16)


def _layer_a_kernel(x_ref, g_ref, win_ref, cw_ref, cb_ref, gw_ref, lam_ref, wout_ref, o_ref,
                    h_ref, gate_ref, xb_ref, uh_ref, a_ref, b_ref, hs_ref, rowc_ref, ones_ref, *xtm,
                    first):
    tile_m = o_ref.shape[0]
    @pl.when(pl.program_id(0) == 0)
    def _():
        xb_ref[0:HALO_A, :] = jnp.zeros((HALO_A, WIDTH), jnp.float32)
        hs_ref[...] = jnp.zeros_like(hs_ref)
        nl = -lam_ref[...]
        softplus = jnp.maximum(nl, 0.0) + jnp.log1p(jnp.exp(-jnp.abs(nl)))
        rowc_ref[0:1, :] = (-0.5 * LRU_C * LOG2_E) * softplus
        rowc_ref[1:1 + CONV_A, :] = 0.5 * cw_ref[...]
        rowc_ref[1 + CONV_A:2 + CONV_A, :] = 0.5 * cb_ref[...]
        lane = lax.broadcasted_iota(jnp.int32, ones_ref.shape, 1)
        ones_ref[...] = jnp.where(lane < 2, 1.0, 0.0)

    if first:
        x_tiles_ref, sems = xtm
        step, n_steps = pl.program_id(0), pl.num_programs(0)
        slot = lax.rem(step, 2)

        @pl.when(step == 0)
        def _():
            for copy in _tile_fetches(x_ref, x_tiles_ref, sems, 0, 0):
                copy.start()

        @pl.when(step + 1 < n_steps)
        def _():
            for copy in _tile_fetches(x_ref, x_tiles_ref, sems, step + 1, 1 - slot):
                copy.start()

        for copy in _tile_fetches(x_ref, x_tiles_ref, sems, step, slot):
            copy.wait()
        x_rows_ref = x_tiles_ref.at[slot].reshape(tile_m, D_MODEL)
    else:
        x_rows_ref = x_ref
    _rmsnorm_rows(x_rows_ref, g_ref, h_ref)

    def in_proj(c):
        lo = c * CHUNK
        gate_ref[:, lo:lo + CHUNK] = _dot(h_ref[...], win_ref[:, lo:lo + CHUNK])
        xb_ref[HALO_A:, lo:lo + CHUNK] = _dot(h_ref[...], win_ref[:, WIDTH + lo:WIDTH + lo + CHUNK])

    def conv(c):
        for hl in range(c * CHUNK, (c + 1) * CHUNK, LRU_BLOCK):
            cols = slice(hl, hl + LRU_BLOCK)
            for r0 in range(0, tile_m, BLOCK_ROWS):
                win = xb_ref[r0:r0 + BLOCK_ROWS + HALO_A, cols]
                acc = win[0:BLOCK_ROWS] * rowc_ref[1:2, cols]
                for k in range(1, CONV_A):
                    acc = acc + win[k * BATCH:k * BATCH + BLOCK_ROWS] * rowc_ref[1 + k:2 + k, cols]
                uh_ref[r0:r0 + BLOCK_ROWS, cols] = acc + rowc_ref[1 + CONV_A:2 + CONV_A, cols]
            xb_ref[0:HALO_A, cols] = xb_ref[tile_m:tile_m + HALO_A, cols]

    def gate_proj(c):
        for hl in range(c * CHUNK, (c + 1) * CHUNK, LRU_BLOCK):
            cols = slice(hl, hl + LRU_BLOCK)
            lhs = jnp.concatenate([uh_ref[:, cols], ones_ref[...]], axis=1)
            g2 = _dot(lhs, gw_ref[hl // LRU_BLOCK])
            a_ref[:, cols] = g2[:, :LRU_BLOCK]
            b_ref[:, cols] = g2[:, LRU_BLOCK:]

    def gates(c):
        for hl in range(c * CHUNK, (c + 1) * CHUNK, LRU_BLOCK):
            cols = slice(hl, hl + LRU_BLOCK)
            for r0 in range(0, tile_m, BLOCK_ROWS):
                rows = slice(r0, r0 + BLOCK_ROWS)
                t_r = jnp.tanh(a_ref[rows, cols])
                t_i = jnp.tanh(b_ref[rows, cols])
                k2 = rowc_ref[0:1, cols]
                log2_a = t_r * k2 + k2
                a = jnp.exp2(log2_a)
                m2 = jnp.tanh(log2_a * (-LN_2)) * (1.0 + a * a)
                mult = m2 * lax.rsqrt(jnp.maximum(m2, F32_MIN_NORMAL))
                uh = uh_ref[rows, cols]
                a_ref[rows, cols] = a
                b_ref[rows, cols] = mult * (t_i * uh + uh)
                gate_ref[rows, cols] = _silu_from_half(gate_ref[rows, cols])

    n_chunks = WIDTH // CHUNK
    in_proj(0)
    conv(0)
    gate_proj(0)
    for c in range(n_chunks):
        if c + 1 < n_chunks:
            in_proj(c + 1)
        gates(c)
        if c + 1 < n_chunks:
            conv(c + 1)
            gate_proj(c + 1)

    hs = hs_ref[...]
    for t in range(tile_m // BATCH):
        rows = slice(t * BATCH, (t + 1) * BATCH)
        hs = a_ref[rows, :] * hs + b_ref[rows, :]
        b_ref[rows, :] = hs * gate_ref[rows, :]
    hs_ref[...] = hs

    o_ref[...] = x_rows_ref[...] + _dot(b_ref[...], wout_ref[...])


def _layer_b_kernel(x_ref, g_ref, win_ref, cw_ref, wout_ref, fg_ref, o_ref, p_ref, z_ref, *rows_scratch,
                    final):
    sub_m = z_ref.shape[0]

    @pl.when(pl.program_id(0) == 0)
    def _():
        p_ref[0:HALO_B, :] = jnp.zeros((HALO_B, WIDTH), jnp.float32)

    def mix_rows(s, carry):
        r0 = pl.multiple_of(s * sub_m, sub_m)
        x = x_ref[pl.ds(r0, sub_m), :]
        h = _rmsnorm(x, g_ref[...])

        for c in range(WIDTH // CHUNK):
            lo = c * CHUNK
            bg = _dot(h, win_ref[:, lo:lo + CHUNK])
            cg = _dot(h, win_ref[:, WIDTH + lo:WIDTH + lo + CHUNK])
            xv = _dot(h, win_ref[:, 2 * WIDTH + lo:2 * WIDTH + lo + CHUNK])
            gate = _dot(h, win_ref[:, 3 * WIDTH + lo:3 * WIDTH + lo + CHUNK])
            p_ref[HALO_B:, lo:lo + CHUNK] = cg * xv
            conv = p_ref[0:sub_m, lo:lo + CHUNK] * cw_ref[0:1, lo:lo + CHUNK]
            for k in range(1, CONV_B):
                conv = conv + p_ref[k * BATCH:k * BATCH + sub_m, lo:lo + CHUNK] * cw_ref[k:k + 1, lo:lo + CHUNK]
            p_ref[0:HALO_B, lo:lo + CHUNK] = p_ref[sub_m:sub_m + HALO_B, lo:lo + CHUNK]
            z_ref[:, lo:lo + CHUNK] = (bg * conv) * _silu_from_half(0.5 * gate)

        out = x + _dot(z_ref[...], wout_ref[...])
        if not final:
            o_ref[pl.ds(r0, sub_m), :] = out
        else:
            (rows_ref,) = rows_scratch
            rows_ref[...] = out
            t_base = pl.multiple_of(s * (sub_m // BATCH), BATCH)
            for t0 in range(0, sub_m // BATCH, BATCH):
                rows = slice(t0 * BATCH, (t0 + BATCH) * BATCH)
                v = rows_ref[rows, :]
                inv = lax.rsqrt(jnp.mean(v * v, axis=-1, keepdims=True) + NORM_EPS)
                for lo in range(0, D_MODEL, LRU_BLOCK):
                    y = rows_ref[rows, lo:lo + LRU_BLOCK] * inv * fg_ref[:, lo:lo + LRU_BLOCK]
                    _store_batch_major(y, o_ref, t_base + t0, lo)
        return carry

    lax.fori_loop(0, x_ref.shape[0] // sub_m, mix_rows, None)


def _resident(shape, *index):
    block = (None,) * len(index) + tuple(shape[len(index):])
    full_index = tuple(index) + (0,) * (len(shape) - len(index))
    return pl.BlockSpec(block, lambda i: full_index, pipeline_mode=pl.Buffered(1))


def _time_major_tile(tile_t):
    return pl.BlockSpec((tile_t * BATCH, D_MODEL), lambda i: (i, 0))


def _batch_major_tile(tile_t):
    return pl.BlockSpec((BATCH, tile_t, D_MODEL), lambda i: (0, i, 0))


def _compiler_params(n_axes=1):
    return pltpu.CompilerParams(dimension_semantics=("arbitrary",) * n_axes,
                                vmem_limit_bytes=VMEM_LIMIT_BYTES)


def _wide_scratch(tile_m):
    return pltpu.VMEM((tile_m, WIDTH), jnp.float32)


def _prep_a(w_in, w_out, gw, gb):
    n_a = w_in.shape[0]
    heads = LRU_HEADS // PREP_SLABS

    def slab(rows, cols):
        return pl.BlockSpec((None, rows // PREP_SLABS, cols), lambda l, s: (l, s, 0))

    return pl.pallas_call(
        _prep_a_kernel,
        grid=(n_a, PREP_SLABS),
        in_specs=[slab(D_MODEL, 2 * WIDTH), slab(WIDTH, D_MODEL),
                  pl.BlockSpec((None, heads, LRU_BLOCK, 2 * LRU_BLOCK), lambda l, s: (l, s, 0, 0)),
                  pl.BlockSpec((None, heads, 1, 2 * LRU_BLOCK), lambda l, s: (l, s, 0, 0))],
        out_specs=[slab(D_MODEL, 2 * WIDTH), slab(WIDTH, D_MODEL),
                   pl.BlockSpec((None, heads, GATE_K, 2 * LRU_BLOCK), lambda l, s: (l, s, 0, 0))],
        out_shape=[jax.ShapeDtypeStruct(w_in.shape, jnp.bfloat16),
                   jax.ShapeDtypeStruct(w_out.shape, jnp.bfloat16),
                   jax.ShapeDtypeStruct((n_a, LRU_HEADS, GATE_K, 2 * LRU_BLOCK), jnp.bfloat16)],
        compiler_params=_compiler_params(2),
        name="rglru_weight_prep",
    )(w_in, w_out, gw, gb.reshape(n_a, LRU_HEADS, 1, 2 * LRU_BLOCK))


def _layer_a(x, j, layer, norm_g, w_in, cw, cb, gw, lam, w_out, first):
    seq = x.shape[1] if first else x.shape[0] // BATCH
    tile_t, tile_m = TILE_T_A, TILE_T_A * BATCH
    scratch = [pltpu.VMEM((tile_m, D_MODEL), jnp.float32),
               _wide_scratch(tile_m),
               pltpu.VMEM((HALO_A + tile_m, WIDTH), jnp.float32),
               _wide_scratch(tile_m),
               _wide_scratch(tile_m),
               _wide_scratch(tile_m),
               pltpu.VMEM((BATCH, WIDTH), jnp.float32),
               pltpu.VMEM((BATCH, WIDTH), jnp.float32),
               pltpu.VMEM((tile_m, GATE_K - LRU_BLOCK), jnp.float32)]
    if first:
        scratch += [pltpu.VMEM((2, tile_t, BATCH, D_MODEL), jnp.float32),
                    pltpu.SemaphoreType.DMA((2, BATCH))]
    return pl.pallas_call(
        functools.partial(_layer_a_kernel, first=first),
        grid=(seq // tile_t,),
        in_specs=[pl.BlockSpec(memory_space=pl.ANY) if first else _time_major_tile(tile_t),
                  _resident(norm_g.shape, layer), _resident(w_in.shape, j), _resident(cw.shape, j),
                  _resident(cb.shape, j), _resident(gw.shape, j),
                  _resident(lam.shape, j), _resident(w_out.shape, j)],
        out_specs=_time_major_tile(tile_t),
        out_shape=jax.ShapeDtypeStruct((seq * BATCH, D_MODEL), x.dtype),
        scratch_shapes=scratch,
        compiler_params=_compiler_params(),
        name="rglru_layer",
    )(x, norm_g, w_in, cw, cb, gw, lam, w_out)


def _layer_b(x, j, layer, norm_g, w_in, cw, w_out, fg, final):
    seq = x.shape[0] // BATCH
    tile_t, tile_m = TILE_T_B, TILE_T_B * BATCH
    out_shape = (BATCH, seq, D_MODEL) if final else x.shape
    sub_m = SUB_T_B * BATCH
    scratch = [pltpu.VMEM((HALO_B + sub_m, WIDTH), jnp.float32),
               _wide_scratch(sub_m)]
    if final:
        scratch.append(pltpu.VMEM((sub_m, D_MODEL), jnp.float32))
    return pl.pallas_call(
        functools.partial(_layer_b_kernel, final=final),
        grid=(seq // tile_t,),
        in_specs=[_time_major_tile(tile_t), _resident(norm_g.shape, layer), _resident(w_in.shape, j),
                  _resident(cw.shape, j), _resident(w_out.shape, j), _resident(fg.shape)],
        out_specs=_batch_major_tile(tile_t) if final else _time_major_tile(tile_t),
        out_shape=jax.ShapeDtypeStruct(out_shape, x.dtype),
        scratch_shapes=scratch,
        compiler_params=_compiler_params(),
        name="shortconv_layer",
    )(x, norm_g, w_in, cw, w_out, fg)


def kernel(x, norm_g, a_w_in, a_conv_w, a_conv_b, a_gate_w, a_gate_b, a_lambda, a_w_out,
           b_w_in, b_conv_w, b_w_out, final_g):
    bsz, seq, d = x.shape
    assert (bsz, d) == (BATCH, D_MODEL) and seq % TILE_T_A == 0 and seq % TILE_T_B == 0 and TILE_T_B % SUB_T_B == 0 and DEPTH % 2 == 0
    n_a = a_w_in.shape[0]
    norm_g = norm_g.reshape(DEPTH, 1, d)
    a_conv_b = a_conv_b.reshape(n_a, 1, WIDTH)
    a_lambda = a_lambda.reshape(n_a, 1, WIDTH)
    fg = final_g.reshape(1, d)
    a_w_in16, a_w_out16, a_gate_w16 = _prep_a(a_w_in, a_w_out, a_gate_w, a_gate_b)
    for i in range(DEPTH):
        j = i // 2
        if i % 2 == 0:
            x = _layer_a(x, j, i, norm_g, a_w_in16, a_conv_w, a_conv_b, a_gate_w16, a_lambda,
                         a_w_out16, first=i == 0)
        else:
            x = _layer_b(x, j, i, norm_g, b_w_in, b_conv_w, b_w_out, fg, final=i == DEPTH - 1)
    return x
```

```python
import functools

import jax
import jax.numpy as jnp
from jax import lax
from jax.experimental import pallas as pl
from jax.experimental.pallas import tpu as pltpu

D_MODEL = 1024
BATCH = 8
DEPTH = 4
WIDTH = 3 * D_MODEL // 2
LRU_HEADS = 12
LRU_BLOCK = WIDTH // LRU_HEADS
CONV_A = 4
CONV_B = 3
LRU_C = 8.0
NORM_EPS = 1e-6

TILE_T_A = 128
TILE_T_B = 128
SUB_T_B = 64
CHUNK = 256
HALO_A = (CONV_A - 1) * BATCH
HALO_B = (CONV_B - 1) * BATCH
BLOCK_ROWS = 64
NORM_ROWS = 32
GATE_K = 2 * LRU_BLOCK
PREP_SLABS = 4
F32_MIN_NORMAL = 1.1754943508222875e-38
LOG2_E = 1.4426950408889634
LN_2 = 0.6931471805599453
VMEM_LIMIT_BYTES = 56 * 1024 * 1024


def _dot(a, b):
    return lax.dot_general(a, b, (((1,), (0,)), ((), ())), preferred_element_type=jnp.float32)


def _rmsnorm(x, g):
    inv = lax.rsqrt(jnp.mean(x * x, axis=-1, keepdims=True) + NORM_EPS)
    return x * inv * g


def _silu_from_half(gh):
    return gh * jnp.tanh(gh) + gh


def _rmsnorm_rows(x_ref, g_ref, dst_ref):
    for r0 in range(0, dst_ref.shape[0], NORM_ROWS):
        rows = slice(r0, r0 + NORM_ROWS)
        dst_ref[rows, :] = _rmsnorm(x_ref[rows, :], g_ref[...])


def _tile_fetches(x_hbm_ref, x_tiles_ref, sems, tile, slot):
    tile_t = x_tiles_ref.shape[1]
    return [pltpu.make_async_copy(x_hbm_ref.at[b, pl.ds(tile * tile_t, tile_t), :],
                                  x_tiles_ref.at[slot, :, b, :], sems.at[slot, b])
            for b in range(BATCH)]


def _tile_stores(rows_ref, o_hbm_ref, sems, tile, slot):
    tile_t = rows_ref.shape[1]
    return [pltpu.make_async_copy(rows_ref.at[slot, :, b, :],
                                  o_hbm_ref.at[b, pl.ds(tile * tile_t, tile_t), :], sems.at[slot, b])
            for b in range(BATCH)]


def _prep_a_kernel(win_ref, wout_ref, gw_ref, gb_ref, win_o, wout_o, gw_o):
    bf16 = jnp.bfloat16
    win_o[:, :WIDTH] = (0.5 * win_ref[:, :WIDTH]).astype(bf16)
    win_o[:, WIDTH:] = win_ref[:, WIDTH:].astype(bf16)
    wout_o[...] = wout_ref[...].astype(bf16)
    heads = gw_ref.shape[0]
    pad_rows = GATE_K - LRU_BLOCK
    row = lax.broadcasted_iota(jnp.int32, (pad_rows, 2 * LRU_BLOCK), 0)
    for hh in range(heads):
        gw_o[hh, :LRU_BLOCK, :] = gw_ref[hh].astype(bf16)
        half_b = 0.5 * gb_ref[hh]
        hi = half_b.astype(bf16).astype(jnp.float32)
        lo = half_b - hi
        gw_o[hh, LRU_BLOCK:, :] = jnp.where(row == 0, hi, jnp.where(row == 1, lo, 0.0)).astype(bf16)


def _layer_a_kernel(x_ref, g_ref, win_ref, cw_ref, cb_ref, gw_ref, lam_ref, wout_ref, o_ref,
                    h_ref, gate_ref, xb_ref, uh_ref, a_ref, b_ref, hs_ref, rowc_ref, ones_ref, *xtm,
                    first):
    tile_m = o_ref.shape[0]
    @pl.when(pl.program_id(0) == 0)
    def _():
        xb_ref[0:HALO_A, :] = jnp.zeros((HALO_A, WIDTH), jnp.float32)
        hs_ref[...] = jnp.zeros_like(hs_ref)
        nl = -lam_ref[...]
        softplus = jnp.maximum(nl, 0.0) + jnp.log1p(jnp.exp(-jnp.abs(nl)))
        rowc_ref[0:1, :] = (-0.5 * LRU_C * LOG2_E) * softplus
        rowc_ref[1:1 + CONV_A, :] = 0.5 * cw_ref[...]
        rowc_ref[1 + CONV_A:2 + CONV_A, :] = 0.5 * cb_ref[...]
        lane = lax.broadcasted_iota(jnp.int32, ones_ref.shape, 1)
        ones_ref[...] = jnp.where(lane < 2, 1.0, 0.0)

    if first:
        x_tiles_ref, sems = xtm
        step, n_steps = pl.program_id(0), pl.num_programs(0)
        slot = lax.rem(step, 2)

        @pl.when(step == 0)
        def _():
            for copy in _tile_fetches(x_ref, x_tiles_ref, sems, 0, 0):
                copy.start()

        @pl.when(step + 1 < n_steps)
        def _():
            for copy in _tile_fetches(x_ref, x_tiles_ref, sems, step + 1, 1 - slot):
                copy.start()

        for copy in _tile_fetches(x_ref, x_tiles_ref, sems, step, slot):
            copy.wait()
        x_rows_ref = x_tiles_ref.at[slot].reshape(tile_m, D_MODEL)
    else:
        x_rows_ref = x_ref
    _rmsnorm_rows(x_rows_ref, g_ref, h_ref)

    def in_proj(c):
        lo = c * CHUNK
        gate_ref[:, lo:lo + CHUNK] = _dot(h_ref[...], win_ref[:, lo:lo + CHUNK])
        xb_ref[HALO_A:, lo:lo + CHUNK] = _dot(h_ref[...], win_ref[:, WIDTH + lo:WIDTH + lo + CHUNK])

    def conv(c):
        for hl in range(c * CHUNK, (c + 1) * CHUNK, LRU_BLOCK):
            cols = slice(hl, hl + LRU_BLOCK)
            for r0 in range(0, tile_m, BLOCK_ROWS):
                win = xb_ref[r0:r0 + BLOCK_ROWS + HALO_A, cols]
                acc = win[0:BLOCK_ROWS] * rowc_ref[1:2, cols]
                for k in range(1, CONV_A):
                    acc = acc + win[k * BATCH:k * BATCH + BLOCK_ROWS] * rowc_ref[1 + k:2 + k, cols]
                uh_ref[r0:r0 + BLOCK_ROWS, cols] = acc + rowc_ref[1 + CONV_A:2 + CONV_A, cols]
            xb_ref[0:HALO_A, cols] = xb_ref[tile_m:tile_m + HALO_A, cols]

    def gate_proj(c):
        for hl in range(c * CHUNK, (c + 1) * CHUNK, LRU_BLOCK):
            cols = slice(hl, hl + LRU_BLOCK)
            lhs = jnp.concatenate([uh_ref[:, cols], ones_ref[...]], axis=1)
            g2 = _dot(lhs, gw_ref[hl // LRU_BLOCK])
            a_ref[:, cols] = g2[:, :LRU_BLOCK]
            b_ref[:, cols] = g2[:, LRU_BLOCK:]

    def gates(c):
        for hl in range(c * CHUNK, (c + 1) * CHUNK, LRU_BLOCK):
            cols = slice(hl, hl + LRU_BLOCK)
            for r0 in range(0, tile_m, BLOCK_ROWS):
                rows = slice(r0, r0 + BLOCK_ROWS)
                t_r = jnp.tanh(a_ref[rows, cols])
                t_i = jnp.tanh(b_ref[rows, cols])
                k2 = rowc_ref[0:1, cols]
                log2_a = t_r * k2 + k2
                a = jnp.exp2(log2_a)
                m2 = jnp.tanh(log2_a * (-LN_2)) * (1.0 + a * a)
                mult = m2 * lax.rsqrt(jnp.maximum(m2, F32_MIN_NORMAL))
                uh = uh_ref[rows, cols]
                a_ref[rows, cols] = a
                b_ref[rows, cols] = mult * (t_i * uh + uh)
                gate_ref[rows, cols] = _silu_from_half(gate_ref[rows, cols])

    n_chunks = WIDTH // CHUNK
    in_proj(0)
    conv(0)
    gate_proj(0)
    for c in range(n_chunks):
        if c + 1 < n_chunks:
            in_proj(c + 1)
        gates(c)
        if c + 1 < n_chunks:
            conv(c + 1)
            gate_proj(c + 1)

    hs = hs_ref[...]
    for t in range(tile_m // BATCH):
        rows = slice(t * BATCH, (t + 1) * BATCH)
        hs = a_ref[rows, :] * hs + b_ref[rows, :]
        b_ref[rows, :] = hs * gate_ref[rows, :]
    hs_ref[...] = hs

    o_ref[...] = x_rows_ref[...] + _dot(b_ref[...], wout_ref[...])


def _layer_b_kernel(x_ref, g_ref, win_ref, cw_ref, wout_ref, fg_ref, o_ref, p_ref, z_ref, *rows_scratch,
                    final):
    sub_m = z_ref.shape[0]

    @pl.when(pl.program_id(0) == 0)
    def _():
        p_ref[0:HALO_B, :] = jnp.zeros((HALO_B, WIDTH), jnp.float32)

    def mix_rows(s, carry):
        r0 = pl.multiple_of(s * sub_m, sub_m)
        x = x_ref[pl.ds(r0, sub_m), :]
        h = _rmsnorm(x, g_ref[...])

        for c in range(WIDTH // CHUNK):
            lo = c * CHUNK
            bg = _dot(h, win_ref[:, lo:lo + CHUNK])
            cg = _dot(h, win_ref[:, WIDTH + lo:WIDTH + lo + CHUNK])
            xv = _dot(h, win_ref[:, 2 * WIDTH + lo:2 * WIDTH + lo + CHUNK])
            gate = _dot(h, win_ref[:, 3 * WIDTH + lo:3 * WIDTH + lo + CHUNK])
            p_ref[HALO_B:, lo:lo + CHUNK] = cg * xv
            conv = p_ref[0:sub_m, lo:lo + CHUNK] * cw_ref[0:1, lo:lo + CHUNK]
            for k in range(1, CONV_B):
                conv = conv + p_ref[k * BATCH:k * BATCH + sub_m, lo:lo + CHUNK] * cw_ref[k:k + 1, lo:lo + CHUNK]
            p_ref[0:HALO_B, lo:lo + CHUNK] = p_ref[sub_m:sub_m + HALO_B, lo:lo + CHUNK]
            z_ref[:, lo:lo + CHUNK] = (bg * conv) * _silu_from_half(0.5 * gate)

        out = x + _dot(z_ref[...], wout_ref[...])
        if not final:
            o_ref[pl.ds(r0, sub_m), :] = out
        else:
            rows_ref, sems = rows_scratch
            k = pl.program_id(0) * n_sub + s
            slot = lax.rem(k, 2)

            @pl.when(k >= 2)
            def _():
                for copy in _tile_stores(rows_ref, o_ref, sems, k - 2, slot):
                    copy.wait()

            rows_ref.at[slot].reshape(sub_m, D_MODEL)[...] = _rmsnorm(out, fg_ref[...])
            for copy in _tile_stores(rows_ref, o_ref, sems, k, slot):
                copy.start()
        return carry

    n_sub = x_ref.shape[0] // sub_m
    lax.fori_loop(0, n_sub, mix_rows, None)

    if final:
        @pl.when(pl.program_id(0) == pl.num_programs(0) - 1)
        def _():
            rows_ref, sems = rows_scratch
            last = pl.num_programs(0) * n_sub - 1
            for k in (last - 1, last):
                for copy in _tile_stores(rows_ref, o_ref, sems, k, lax.rem(k, 2)):
                    copy.wait()


def _resident(shape, *index):
    block = (None,) * len(index) + tuple(shape[len(index):])
    full_index = tuple(index) + (0,) * (len(shape) - len(index))
    return pl.BlockSpec(block, lambda i: full_index, pipeline_mode=pl.Buffered(1))


def _time_major_tile(tile_t):
    return pl.BlockSpec((tile_t * BATCH, D_MODEL), lambda i: (i, 0))


def _batch_major_tile(tile_t):
    return pl.BlockSpec((BATCH, tile_t, D_MODEL), lambda i: (0, i, 0))


def _compiler_params(n_axes=1):
    return pltpu.CompilerParams(dimension_semantics=("arbitrary",) * n_axes,
                                vmem_limit_bytes=VMEM_LIMIT_BYTES)


def _wide_scratch(tile_m):
    return pltpu.VMEM((tile_m, WIDTH), jnp.float32)


def _prep_a(w_in, w_out, gw, gb):
    n_a = w_in.shape[0]
    heads = LRU_HEADS // PREP_SLABS

    def slab(rows, cols):
        return pl.BlockSpec((None, rows // PREP_SLABS, cols), lambda l, s: (l, s, 0))

    return pl.pallas_call(
        _prep_a_kernel,
        grid=(n_a, PREP_SLABS),
        in_specs=[slab(D_MODEL, 2 * WIDTH), slab(WIDTH, D_MODEL),
                  pl.BlockSpec((None, heads, LRU_BLOCK, 2 * LRU_BLOCK), lambda l, s: (l, s, 0, 0)),
                  pl.BlockSpec((None, heads, 1, 2 * LRU_BLOCK), lambda l, s: (l, s, 0, 0))],
        out_specs=[slab(D_MODEL, 2 * WIDTH), slab(WIDTH, D_MODEL),
                   pl.BlockSpec((None, heads, GATE_K, 2 * LRU_BLOCK), lambda l, s: (l, s, 0, 0))],
        out_shape=[jax.ShapeDtypeStruct(w_in.shape, jnp.bfloat16),
                   jax.ShapeDtypeStruct(w_out.shape, jnp.bfloat16),
                   jax.ShapeDtypeStruct((n_a, LRU_HEADS, GATE_K, 2 * LRU_BLOCK), jnp.bfloat16)],
        compiler_params=_compiler_params(2),
        name="rglru_weight_prep",
    )(w_in, w_out, gw, gb.reshape(n_a, LRU_HEADS, 1, 2 * LRU_BLOCK))


def _layer_a(x, j, layer, norm_g, w_in, cw, cb, gw, lam, w_out, first):
    seq = x.shape[1] if first else x.shape[0] // BATCH
    tile_t, tile_m = TILE_T_A, TILE_T_A * BATCH
    scratch = [pltpu.VMEM((tile_m, D_MODEL), jnp.float32),
               _wide_scratch(tile_m),
               pltpu.VMEM((HALO_A + tile_m, WIDTH), jnp.float32),
               _wide_scratch(tile_m),
               _wide_scratch(tile_m),
               _wide_scratch(tile_m),
               pltpu.VMEM((BATCH, WIDTH), jnp.float32),
               pltpu.VMEM((BATCH, WIDTH), jnp.float32),
               pltpu.VMEM((tile_m, GATE_K - LRU_BLOCK), jnp.float32)]
    if first:
        scratch += [pltpu.VMEM((2, tile_t, BATCH, D_MODEL), jnp.float32),
                    pltpu.SemaphoreType.DMA((2, BATCH))]
    return pl.pallas_call(
        functools.partial(_layer_a_kernel, first=first),
        grid=(seq // tile_t,),
        in_specs=[pl.BlockSpec(memory_space=pl.ANY) if first else _time_major_tile(tile_t),
                  _resident(norm_g.shape, layer), _resident(w_in.shape, j), _resident(cw.shape, j),
                  _resident(cb.shape, j), _resident(gw.shape, j),
                  _resident(lam.shape, j), _resident(w_out.shape, j)],
        out_specs=_time_major_tile(tile_t),
        out_shape=jax.ShapeDtypeStruct((seq * BATCH, D_MODEL), x.dtype),
        scratch_shapes=scratch,
        compiler_params=_compiler_params(),
        name="rglru_layer",
    )(x, norm_g, w_in, cw, cb, gw, lam, w_out)


def _layer_b(x, j, layer, norm_g, w_in, cw, w_out, fg, final):
    seq = x.shape[0] // BATCH
    tile_t, tile_m = TILE_T_B, TILE_T_B * BATCH
    out_shape = (BATCH, seq, D_MODEL) if final else x.shape
    sub_m = SUB_T_B * BATCH
    scratch = [pltpu.VMEM((HALO_B + sub_m, WIDTH), jnp.float32),
               _wide_scratch(sub_m)]
    if final:
        scratch += [pltpu.VMEM((2, SUB_T_B, BATCH, D_MODEL), jnp.float32),
                    pltpu.SemaphoreType.DMA((2, BATCH))]
    return pl.pallas_call(
        functools.partial(_layer_b_kernel, final=final),
        grid=(seq // tile_t,),
        in_specs=[_time_major_tile(tile_t), _resident(norm_g.shape, layer), _resident(w_in.shape, j),
                  _resident(cw.shape, j), _resident(w_out.shape, j), _resident(fg.shape)],
        out_specs=pl.BlockSpec(memory_space=pl.ANY) if final else _time_major_tile(tile_t),
        out_shape=jax.ShapeDtypeStruct(out_shape, x.dtype),
        scratch_shapes=scratch,
        compiler_params=_compiler_params(),
        name="shortconv_layer",
    )(x, norm_g, w_in, cw, w_out, fg)


def kernel(x, norm_g, a_w_in, a_conv_w, a_conv_b, a_gate_w, a_gate_b, a_lambda, a_w_out,
           b_w_in, b_conv_w, b_w_out, final_g):
    bsz, seq, d = x.shape
    assert (bsz, d) == (BATCH, D_MODEL) and seq % TILE_T_A == 0 and seq % TILE_T_B == 0 and TILE_T_B % SUB_T_B == 0 and DEPTH % 2 == 0
    n_a = a_w_in.shape[0]
    norm_g = norm_g.reshape(DEPTH, 1, d)
    a_conv_b = a_conv_b.reshape(n_a, 1, WIDTH)
    a_lambda = a_lambda.reshape(n_a, 1, WIDTH)
    fg = final_g.reshape(1, d)
    a_w_in16, a_w_out16, a_gate_w16 = _prep_a(a_w_in, a_w_out, a_gate_w, a_gate_b)
    for i in range(DEPTH):
        j = i // 2
        if i % 2 == 0:
            x = _layer_a(x, j, i, norm_g, a_w_in16, a_conv_w, a_conv_b, a_gate_w16, a_lambda,
                         a_w_out16, first=i == 0)
        else:
            x = _layer_b(x, j, i, norm_g, b_w_in, b_conv_w, b_w_out, fg, final=i == DEPTH - 1)
    return x
```
